```python
import math
import jax, jax.numpy as jnp
from jax import lax
import numpy as np

D_MODEL = 2048
BATCH = 2
SEQ = 8192
DEPTH = 4

N_MIXERS = 3
ROPE_THETA = 10000.0
EPS = 1e-6
NEG = -1e30

DA_QK_DIM = 128
DA_V_DIM = 2 * DA_QK_DIM
DA_HEADS = D_MODEL // DA_V_DIM
DA_WIDTH = DA_HEADS * DA_V_DIM
DA_QBLOCK = 128
DA_IN = 2 * DA_HEADS * DA_QK_DIM * 2 + DA_WIDTH + DA_WIDTH

SW_HEAD_DIM = 64
SW_Q_HEADS = D_MODEL // SW_HEAD_DIM
SW_KV_HEADS = SW_Q_HEADS // 8
SW_GROUP = SW_Q_HEADS // SW_KV_HEADS
SW_WIDTH = SW_Q_HEADS * SW_HEAD_DIM
WINDOW = 128
SW_IN = SW_WIDTH + 2 * SW_KV_HEADS * SW_HEAD_DIM + SW_WIDTH

ML_HEADS = 8
ML_V_DIM = D_MODEL // ML_HEADS
ML_QK_DIM = ML_V_DIM // 2
ML_WIDTH = ML_HEADS * ML_V_DIM
ML_CHUNK = 64
ML_IN = 2 * ML_HEADS * ML_QK_DIM + ML_WIDTH + ML_WIDTH + 2 * ML_HEADS + ML_WIDTH

N_A = (DEPTH + 2) // 3
N_B = (DEPTH + 1) // 3
N_C = DEPTH // 3

kernel_name = "hybrid_diffattn_swa_sink_mlstm_gated"


def rmsnorm(x, g):
    xf = x.astype(jnp.float32)
    y = xf * lax.rsqrt(jnp.mean(xf * xf, axis=-1, keepdims=True) + EPS)
    return (y * g.astype(jnp.float32)).astype(x.dtype)


def rope_tables(positions, dim):
    inv_freq = ROPE_THETA ** (-jnp.arange(0, dim, 2, dtype=jnp.float32) / dim)
    ang = positions.astype(jnp.float32)[..., None] * inv_freq
    return jnp.cos(ang)[:, :, None, :], jnp.sin(ang)[:, :, None, :]


def rope(x, cos, sin):
    half = x.shape[-1] // 2
    x1, x2 = x[..., :half], x[..., half:]
    return jnp.concatenate([x1 * cos - x2 * sin, x2 * cos + x1 * sin], axis=-1).astype(x.dtype)


def diff_attention(h, w_in, w_out, lam, subln_g, cos, sin, lambda_init):
    B, S, _ = h.shape
    H, d, dv = DA_HEADS, DA_QK_DIM, DA_V_DIM
    nq = 2 * H * d
    proj = h @ w_in
    q, k, v, g = jnp.split(proj, [nq, 2 * nq, 2 * nq + DA_WIDTH], axis=-1)
    q = rope(q.reshape(B, S, 2 * H, d), cos, sin) * (d ** -0.5)
    k = rope(k.reshape(B, S, 2 * H, d), cos, sin)
    q = q.reshape(B, S, H, 2, d).transpose(0, 2, 3, 1, 4)
    k = k.reshape(B, S, H, 2, d).transpose(0, 2, 3, 1, 4)
    v = v.reshape(B, S, H, dv).transpose(0, 2, 1, 3)
    lamf = lam.astype(jnp.float32)
    lam_full = (jnp.exp(jnp.sum(lamf[0] * lamf[1])) - jnp.exp(jnp.sum(lamf[2] * lamf[3])) + lambda_init)
    nb = S // DA_QBLOCK
    qb = q.reshape(B, H, 2, nb, DA_QBLOCK, d).transpose(3, 0, 1, 2, 4, 5)
    kpos = jnp.arange(S)

    def attend(args):
        qblk, blk = args
        s = jnp.einsum('bhmqd,bhmkd->bhmqk', qblk, k, preferred_element_type=jnp.float32)
        qpos = blk * DA_QBLOCK + jnp.arange(DA_QBLOCK)
        mask = kpos[None, :] <= qpos[:, None]
        p = jax.nn.softmax(jnp.where(mask, s, NEG), axis=-1)
        pd = p[:, :, 0] - lam_full * p[:, :, 1]
        return jnp.einsum('bhqk,bhkv->bhqv', pd.astype(v.dtype), v, preferred_element_type=jnp.float32)

    o = lax.map(attend, (qb, jnp.arange(nb)))
    o = o.transpose(1, 0, 3, 2, 4).reshape(B, S, H, dv)
    o = rmsnorm(o, subln_g) * (1.0 - lambda_init)
    o = o.reshape(B, S, DA_WIDTH) * jax.nn.silu(g.astype(jnp.float32))
    return o.astype(h.dtype) @ w_out


def sliding_window_attention(h, w_in, w_out, sinks, cos, sin):
    B, S, _ = h.shape
    d, KV, G, W = SW_HEAD_DIM, SW_KV_HEADS, SW_GROUP, WINDOW
    nkv = KV * d
    proj = h @ w_in
    q, k, v, g = jnp.split(proj, [SW_WIDTH, SW_WIDTH + nkv, SW_WIDTH + 2 * nkv], axis=-1)
    q = rope(q.reshape(B, S, SW_Q_HEADS, d), cos, sin) * (d ** -0.5)
    k = rope(k.reshape(B, S, KV, d), cos, sin)
    v = v.reshape(B, S, KV, d)
    nb = S // W
    qb = q.reshape(B, nb, W, KV, G, d)
    kb = k.reshape(B, nb, W, KV, d)
    vb = v.reshape(B, nb, W, KV, d)
    pad = ((0, 0), (1, 0), (0, 0), (0, 0), (0, 0))
    kk = jnp.concatenate([jnp.pad(kb, pad)[:, :-1], kb], axis=2)
    vv = jnp.concatenate([jnp.pad(vb, pad)[:, :-1], vb], axis=2)
    s = jnp.einsum('bnqhgd,bnkhd->bhgnqk', qb, kk, preferred_element_type=jnp.float32)
    qrel = jnp.arange(W)[:, None] + W
    krel = jnp.arange(2 * W)[None, :]
    band = (krel <= qrel) & (qrel - krel < W)
    has_prev = (jnp.arange(nb)[:, None, None] > 0) | (krel[None] >= W)
    mask = band[None] & has_prev
    s = jnp.where(mask[None, None, None], s, NEG)
    sk = sinks.astype(jnp.float32).reshape(KV, G)[None, :, :, None, None, None]
    m = jnp.maximum(jnp.max(s, axis=-1, keepdims=True), sk)
    e = jnp.exp(s - m)
    p = e / (jnp.sum(e, axis=-1, keepdims=True) + jnp.exp(sk - m))
    o = jnp.einsum('bhgnqk,bnkhd->bnqhgd', p.astype(vv.dtype), vv, preferred_element_type=jnp.float32)
    o = o.reshape(B, S, SW_WIDTH) * jax.nn.silu(g.astype(jnp.float32))
    return o.astype(h.dtype) @ w_out


def mlstm(h, w_in, b_gates, w_out, norm_g):
    B, S, _ = h.shape
    H, dk, dv, L = ML_HEADS, ML_QK_DIM, ML_V_DIM, ML_CHUNK
    nqk = H * dk
    proj = h @ w_in
    q, k, v, o, gif, g = jnp.split(
        proj, [nqk, 2 * nqk, 2 * nqk + ML_WIDTH, 2 * nqk + 2 * ML_WIDTH, 2 * nqk + 2 * ML_WIDTH + 2 * H], axis=-1)
    nc = S // L

    def chunks(t, dim):
        return t.reshape(B, nc, L, H, dim).transpose(1, 0, 3, 2, 4)

    qc = chunks(q * (dk ** -0.5), dk)
    kc = chunks(k, dk)
    vc = chunks(v, dv)
    gates = (gif.astype(jnp.float32) + b_gates.astype(jnp.float32)).reshape(B, S, 2, H)
    i_pre = gates[:, :, 0].reshape(B, nc, L, H).transpose(1, 0, 3, 2)
    log_f = jax.nn.log_sigmoid(gates[:, :, 1]).reshape(B, nc, L, H).transpose(1, 0, 3, 2)
    causal = jnp.tril(jnp.ones((L, L), dtype=bool))

    def step(carry, xs):
        C, n, m = carry
        qx, kx, vx, ix, fx = xs
        b = jnp.cumsum(fx, axis=-1)
        logw = jnp.where(causal, b[..., :, None] - b[..., None, :] + ix[..., None, :], NEG)
        log_inter = b + m[..., None]
        m_t = jnp.maximum(log_inter, jnp.max(logw, axis=-1))
        w_intra = jnp.exp(logw - m_t[..., None])
        w_inter = jnp.exp(log_inter - m_t)
        a = jnp.einsum('bhtd,bhsd->bhts', qx, kx, preferred_element_type=jnp.float32) * w_intra
        num = (jnp.einsum('bhts,bhsv->bhtv', a, vx)
               + w_inter[..., None] * jnp.einsum('bhvd,bhtd->bhtv', C, qx))
        den = jnp.sum(a, axis=-1) + w_inter * jnp.einsum('bhd,bhtd->bht', n, qx)
        hx = num / jnp.maximum(jnp.abs(den), jnp.exp(-m_t))[..., None]
        m_new = m_t[..., -1]
        wk = jnp.exp(b[..., -1:] - b + ix - m_new[..., None])
        decay = jnp.exp(b[..., -1] + m - m_new)
        C_new = decay[..., None, None] * C + jnp.einsum('bhs,bhsv,bhsd->bhvd', wk, vx, kx)
        n_new = decay[..., None] * n + jnp.einsum('bhs,bhsd->bhd', wk, kx)
        return (C_new, n_new, m_new), hx

    init = (jnp.zeros((B, H, dv, dk), jnp.float32), jnp.zeros((B, H, dk), jnp.float32),
            jnp.zeros((B, H), jnp.float32))
    _, hs = lax.scan(step, init, (qc, kc, vc, i_pre, log_f))
    hs = hs.transpose(1, 0, 3, 2, 4).reshape(B, S, H, dv)
    hs = rmsnorm(hs, norm_g).reshape(B, S, ML_WIDTH)
    out = hs * jax.nn.sigmoid(o.astype(jnp.float32)) * jax.nn.silu(g.astype(jnp.float32))
    return out.astype(h.dtype) @ w_out


def setup_inputs(seed: int = 0) -> dict:
    key = jax.random.key(seed)
    ks = jax.random.split(key, 16)
    f32 = jnp.float32
    nrm = lambda k, shape, s: jax.random.normal(k, shape, f32) * s
    x = jax.random.normal(ks[0], (BATCH, SEQ, D_MODEL), f32)
    positions = jnp.broadcast_to(jnp.arange(SEQ, dtype=jnp.int32), (BATCH, SEQ))
    norm_g = 1.0 + nrm(ks[1], (DEPTH, D_MODEL), 0.02)
    final_g = 1.0 + nrm(ks[2], (D_MODEL,), 0.02)
    da_w_in = nrm(ks[3], (N_A, D_MODEL, DA_IN), D_MODEL ** -0.5)
    da_w_out = nrm(ks[4], (N_A, DA_WIDTH, D_MODEL), DA_WIDTH ** -0.5)
    da_lambda = nrm(ks[5], (N_A, 4, DA_QK_DIM), 0.1)
    da_subln_g = 1.0 + nrm(ks[6], (N_A, DA_V_DIM), 0.02)
    sw_w_in = nrm(ks[7], (N_B, D_MODEL, SW_IN), D_MODEL ** -0.5)
    sw_w_out = nrm(ks[8], (N_B, SW_WIDTH, D_MODEL), SW_WIDTH ** -0.5)
    sw_sinks = nrm(ks[9], (N_B, SW_Q_HEADS), 0.5)
    ml_w_in = nrm(ks[10], (N_C, D_MODEL, ML_IN), D_MODEL ** -0.5)
    ib = nrm(ks[11], (N_C, ML_HEADS), 0.1)
    fb = 3.0 + nrm(ks[12], (N_C, ML_HEADS), 0.5)
    ml_b_gates = jnp.concatenate([ib, fb], axis=-1)
    ml_w_out = nrm(ks[13], (N_C, ML_WIDTH, D_MODEL), ML_WIDTH ** -0.5)
    ml_norm_g = 1.0 + nrm(ks[14], (N_C, ML_V_DIM), 0.02)
    return {"x": x, "positions": positions, "norm_g": norm_g, "final_g": final_g,
            "da_w_in": da_w_in, "da_w_out": da_w_out, "da_lambda": da_lambda, "da_subln_g": da_subln_g,
            "sw_w_in": sw_w_in, "sw_w_out": sw_w_out, "sw_sinks": sw_sinks,
            "ml_w_in": ml_w_in, "ml_b_gates": ml_b_gates, "ml_w_out": ml_w_out, "ml_norm_g": ml_norm_g}


def reference(x, positions, norm_g, final_g, da_w_in, da_w_out, da_lambda, da_subln_g,
              sw_w_in, sw_w_out, sw_sinks, ml_w_in, ml_b_gates, ml_w_out, ml_norm_g):
    cos_a, sin_a = rope_tables(positions, DA_QK_DIM)
    cos_b, sin_b = rope_tables(positions, SW_HEAD_DIM)
    for i in range(DEPTH):
        hn = rmsnorm(x, norm_g[i])
        kind, j = i % N_MIXERS, i // N_MIXERS
        if kind == 0:
            lambda_init = 0.8 - 0.6 * math.exp(-0.3 * i)
            y = diff_attention(hn, da_w_in[j], da_w_out[j], da_lambda[j], da_subln_g[j],
                               cos_a, sin_a, lambda_init)
        elif kind == 1:
            y = sliding_window_attention(hn, sw_w_in[j], sw_w_out[j], sw_sinks[j], cos_b, sin_b)
        else:
            y = mlstm(hn, ml_w_in[j], ml_b_gates[j], ml_w_out[j], ml_norm_g[j])
        x = x + y.astype(x.dtype)
    return rmsnorm(x, final_g)
```

```python
import functools
import math

import jax
import jax.numpy as jnp
from jax import lax
from jax.experimental import pallas as pl
from jax.experimental.pallas import tpu as pltpu

F32 = jnp.float32
BF16 = jnp.bfloat16

ROPE_THETA = 10000.0
EPS = 1e-6
NEG = -1e30
N_MIXERS = 3

LANES = 128
VMEM_LIMIT = 56 * 1024 * 1024

DA_QK_DIM = 128
DA_V_DIM = 256
SW_HEAD_DIM = 64
SW_GROUP = 8
SW_WINDOW = 128
ML_QK_DIM = 128
ML_V_DIM = 256


def _cparams(sem):
    return pltpu.CompilerParams(dimension_semantics=sem, vmem_limit_bytes=VMEM_LIMIT)


def _tile(n, pref):
    t = min(n, pref)
    assert n % t == 0, (n, t)
    return t


def _norm_proj_kernel(x_ref, g_ref, w_ref, cos_ref, sa_ref, sb_ref, o_ref, hn_ref, *,
                      n_rope_tiles, shift_a, shift_b):
    j = pl.program_id(1)

    @pl.when(j == 0)
    def _():
        x = x_ref[...]
        ms = jnp.mean(x * x, axis=-1, keepdims=True)
        hn_ref[...] = (x * lax.rsqrt(ms + EPS) * g_ref[...]).astype(BF16)

    acc = jnp.dot(hn_ref[...], w_ref[...], preferred_element_type=F32)
    tn = acc.shape[1]

    if n_rope_tiles == 0:
        o_ref[...] = acc.astype(o_ref.dtype)
        return

    @pl.when(j < n_rope_tiles)
    def _():
        cos = cos_ref[0]
        sa = sa_ref[0]
        sb = sb_ref[0]
        for c in range(tn // LANES):
            xc = acc[:, c * LANES:(c + 1) * LANES]
            out = xc * cos + pltpu.roll(xc, shift_a, 1) * sa
            if shift_b != shift_a:
                out = out + pltpu.roll(xc, shift_b, 1) * sb
            o_ref[:, c * LANES:(c + 1) * LANES] = out.astype(o_ref.dtype)

    @pl.when(j >= n_rope_tiles)
    def _():
        o_ref[...] = acc.astype(o_ref.dtype)


def _norm_proj(x2d, g, w, tabs, *, tm, tn, n_rope_tiles=0, nq_tiles=0, shift_a=0, shift_b=0,
               out_dtype=BF16):
    n, d = x2d.shape
    dout = w.shape[1]
    tm = _tile(n, tm)
    assert dout % tn == 0
    cos, sa, sb = tabs

    def tab_map(i, j):
        return (jnp.where(j < nq_tiles, 0, 1), i, 0)

    tab_spec = pl.BlockSpec((1, tm, LANES), tab_map)
    kern = functools.partial(_norm_proj_kernel, n_rope_tiles=n_rope_tiles, shift_a=shift_a, shift_b=shift_b)
    return pl.pallas_call(
        kern,
        out_shape=jax.ShapeDtypeStruct((n, dout), out_dtype),
        grid=(n // tm, dout // tn),
        in_specs=[
            pl.BlockSpec((tm, d), lambda i, j: (i, 0)),
            pl.BlockSpec((1, d), lambda i, j: (0, 0)),
            pl.BlockSpec((d, tn), lambda i, j: (0, j)),
            tab_spec, tab_spec, tab_spec,
        ],
        out_specs=pl.BlockSpec((tm, tn), lambda i, j: (i, j)),
        scratch_shapes=[pltpu.VMEM((tm, d), BF16)],
        compiler_params=_cparams(("parallel", "arbitrary")),
        name="norm_proj",
    )(x2d, g.reshape(1, d), w, cos, sa, sb)


def _out_proj_kernel(o_ref, w_ref, x_ref, fg_ref, out_ref, *, final_norm):
    y = jnp.dot(o_ref[...], w_ref[...], preferred_element_type=F32)
    xn = x_ref[...] + y
    if final_norm:
        ms = jnp.mean(xn * xn, axis=-1, keepdims=True)
        xn = xn * lax.rsqrt(ms + EPS) * fg_ref[...]
    out_ref[...] = xn


def _out_proj(o2d, w, x2d, final_g, *, tm, final_norm):
    n, d = x2d.shape
    k = o2d.shape[1]
    tm = _tile(n, tm)
    return pl.pallas_call(
        functools.partial(_out_proj_kernel, final_norm=final_norm),
        out_shape=jax.ShapeDtypeStruct((n, d), F32),
        grid=(n // tm,),
        in_specs=[
            pl.BlockSpec((tm, k), lambda i: (i, 0)),
            pl.BlockSpec((k, d), lambda i: (0, 0)),
            pl.BlockSpec((tm, d), lambda i: (i, 0)),
            pl.BlockSpec((1, d), lambda i: (0, 0)),
        ],
        out_specs=pl.BlockSpec((tm, d), lambda i: (i, 0)),
        compiler_params=_cparams(("parallel",)),
        name="out_proj",
    )(o2d, w, x2d, final_g.reshape(1, d))


def _silu(g):
    return g * jax.nn.sigmoid(g)


def _da_kernel(lam_ref, sg_ref, q_ref, k_ref, v_ref, g_ref, o_ref, acc1_ref, acc2_ref, *, t, lam_init):
    i = pl.program_id(2)
    d = DA_QK_DIM
    q = q_ref[0]
    qs = (q[:, :d], q[:, d:])
    accs = (acc1_ref, acc2_ref)
    acc1_ref[...] = jnp.zeros_like(acc1_ref)
    acc2_ref[...] = jnp.zeros_like(acc2_ref)

    def step(j, carry, masked):
        off = pl.multiple_of(j * t, t)
        kb = k_ref[0, pl.ds(off, t), :]
        vb = v_ref[0, pl.ds(off, t), :]
        new = []
        for mp in range(2):
            m, l = carry[2 * mp], carry[2 * mp + 1]
            s = lax.dot_general(qs[mp], kb[:, mp * d:(mp + 1) * d], (((1,), (1,)), ((), ())),
                                preferred_element_type=F32)
            if masked:
                row = lax.broadcasted_iota(jnp.int32, s.shape, 0)
                col = lax.broadcasted_iota(jnp.int32, s.shape, 1)
                s = jnp.where(col <= row, s, NEG)
            m_new = jnp.maximum(m, jnp.max(s, axis=-1, keepdims=True))
            alpha = jnp.exp(m - m_new)
            p = jnp.exp(s - m_new)
            l_new = alpha * l + jnp.sum(p, axis=-1, keepdims=True)
            accs[mp][...] = accs[mp][...] * alpha + jnp.dot(p.astype(BF16), vb, preferred_element_type=F32)
            new += [m_new, l_new]
        return tuple(new)

    m0 = jnp.full((t, 1), NEG, F32)
    l0 = jnp.zeros((t, 1), F32)
    carry = lax.fori_loop(0, i, functools.partial(step, masked=False), (m0, l0, m0, l0))
    _, l1, _, l2 = step(i, carry, True)

    lam = lam_ref[...]
    lam_full = (jnp.exp(jnp.sum(lam[0:1] * lam[1:2], axis=-1, keepdims=True))
                - jnp.exp(jnp.sum(lam[2:3] * lam[3:4], axis=-1, keepdims=True)) + lam_init)
    o = acc1_ref[...] * (1.0 / l1) - lam_full * (acc2_ref[...] * (1.0 / l2))
    ms = jnp.mean(o * o, axis=-1, keepdims=True)
    o = o * lax.rsqrt(ms + EPS) * sg_ref[...] * (1.0 - lam_init)
    o = o * _silu(g_ref[0].astype(F32))
    o_ref[0] = o.astype(o_ref.dtype)


def _diff_attention(proj, lam, subln_g, lam_init, *, b, s, heads, t):
    dv = DA_V_DIM
    t = _tile(s, t)
    kern = functools.partial(_da_kernel, t=t, lam_init=lam_init)
    return pl.pallas_call(
        kern,
        out_shape=jax.ShapeDtypeStruct((b, s, heads * dv), BF16),
        grid=(b, heads, s // t),
        in_specs=[
            pl.BlockSpec((4, DA_QK_DIM), lambda bi, h, i: (0, 0)),
            pl.BlockSpec((1, dv), lambda bi, h, i: (0, 0)),
            pl.BlockSpec((1, t, dv), lambda bi, h, i: (bi, i, h)),
            pl.BlockSpec((1, s, dv), lambda bi, h, i: (bi, 0, heads + h)),
            pl.BlockSpec((1, s, dv), lambda bi, h, i: (bi, 0, 2 * heads + h)),
            pl.BlockSpec((1, t, dv), lambda bi, h, i: (bi, i, 3 * heads + h)),
        ],
        out_specs=pl.BlockSpec((1, t, dv), lambda bi, h, i: (bi, i, h)),
        scratch_shapes=[pltpu.VMEM((t, dv), F32), pltpu.VMEM((t, dv), F32)],
        compiler_params=_cparams(("parallel", "parallel", "arbitrary")),
        name="diff_attention",
    )(lam.astype(F32), subln_g.reshape(1, dv).astype(F32), proj, proj, proj, proj)


def _swa_kernel(sink_ref, q_ref, kvc_ref, kvp_ref, g0_ref, g1_ref, g2_ref, g3_ref, o_ref):
    i = pl.program_id(1)
    w = SW_WINDOW
    hd = SW_HEAD_DIM
    nkv = 4
    g_refs = (g0_ref, g1_ref, g2_ref, g3_ref)
    has_prev = i > 0

    lane = lax.broadcasted_iota(jnp.int32, (w, LANES), 1)
    lo = lane < hd
    rows = 4 * w
    row = lax.broadcasted_iota(jnp.int32, (rows, w), 0) % w
    col = lax.broadcasted_iota(jnp.int32, (rows, w), 1)
    lower = col <= row

    def split_pair(x_bf16, parity):
        xf = x_bf16.astype(F32)
        if parity == 0:
            a = jnp.where(lo, xf, 0.0)
            bb = pltpu.roll(a, hd, 1)
        else:
            bb = jnp.where(lo, 0.0, xf)
            a = pltpu.roll(bb, hd, 1)
        return a.astype(BF16), bb.astype(BF16)

    kvc = kvc_ref[0]
    kvp = kvp_ref[0]
    q_all = q_ref[0]
    for kvh in range(nkv):
        c, parity = kvh // 2, kvh % 2
        kc = split_pair(kvc[:, c * LANES:(c + 1) * LANES], parity)
        kp = split_pair(kvp[:, c * LANES:(c + 1) * LANES], parity)
        vc = split_pair(kvc[:, (2 + c) * LANES:(3 + c) * LANES], parity)
        vp = split_pair(kvp[:, (2 + c) * LANES:(3 + c) * LANES], parity)
        base = kvh * SW_GROUP * hd
        q4 = jnp.concatenate([q_all[:, base + jj * LANES: base + (jj + 1) * LANES] for jj in range(4)], axis=0)
        o4 = jnp.zeros((rows, LANES), F32)
        for ab in range(2):
            dn = (((1,), (1,)), ((), ()))
            s_cur = lax.dot_general(q4, kc[ab], dn, preferred_element_type=F32)
            s_prev = lax.dot_general(q4, kp[ab], dn, preferred_element_type=F32)
            s_prev = jnp.where(has_prev, s_prev, NEG)
            s = jnp.where(lower, s_cur, s_prev)
            sink = jnp.concatenate(
                [jnp.full((w, 1), sink_ref[kvh * SW_GROUP + 2 * jj + ab], F32) for jj in range(4)], axis=0)
            m = jnp.maximum(jnp.max(s, axis=-1, keepdims=True), sink)
            e = jnp.exp(s - m)
            denom = jnp.sum(e, axis=-1, keepdims=True) + jnp.exp(sink - m)
            p = e * (1.0 / denom)
            p_cur = jnp.where(lower, p, 0.0).astype(BF16)
            p_prev = jnp.where(lower, 0.0, p).astype(BF16)
            o4 = o4 + jnp.dot(p_cur, vc[ab], preferred_element_type=F32)
            o4 = o4 + jnp.dot(p_prev, vp[ab], preferred_element_type=F32)
        gate = g_refs[kvh][0].astype(F32)
        for jj in range(4):
            og = o4[jj * w:(jj + 1) * w] * _silu(gate[:, jj * LANES:(jj + 1) * LANES])
            o_ref[0, :, base + jj * LANES: base + (jj + 1) * LANES] = og.astype(o_ref.dtype)


def _sliding_window_attention(proj, sinks, *, b, s):
    w = SW_WINDOW
    width = 32 * SW_HEAD_DIM
    gw = SW_GROUP * SW_HEAD_DIM
    kv_blk = width // gw
    g_specs = [pl.BlockSpec((1, w, gw), functools.partial(lambda bi, i, kk: (bi, i, kk), kk=kv_blk + 1 + kvh))
               for kvh in range(4)]
    return pl.pallas_call(
        _swa_kernel,
        out_shape=jax.ShapeDtypeStruct((b, s, width), BF16),
        grid=(b, s // w),
        in_specs=[
            pl.BlockSpec(memory_space=pltpu.SMEM),
            pl.BlockSpec((1, w, width), lambda bi, i: (bi, i, 0)),
            pl.BlockSpec((1, w, gw), lambda bi, i: (bi, i, kv_blk)),
            pl.BlockSpec((1, w, gw), lambda bi, i: (bi, jnp.maximum(i - 1, 0), kv_blk)),
        ] + g_specs,
        out_specs=pl.BlockSpec((1, w, width), lambda bi, i: (bi, i, 0)),
        compiler_params=_cparams(("parallel", "arbitrary")),
        name="sliding_window_attention",
    )(sinks.astype(F32), proj, proj, proj, proj, proj, proj, proj)


def _log_sigmoid(x):
    return jnp.minimum(x, 0.0) - jnp.log1p(jnp.exp(-jnp.abs(x)))


def _split3(x):
    hi = x.astype(BF16)
    r1 = x - hi.astype(F32)
    mid = r1.astype(BF16)
    lo = (r1 - mid.astype(F32)).astype(BF16)
    return hi, mid, lo


def _mlstm_kernel(gcol_ref, grow_ref, bcol_ref, brow_ref, ng_ref, q_ref, k_ref, v_ref, og_ref, sgate_ref,
                  o_ref, ct_ref, n_ref, m_ref, *, chunk, heads):
    h = pl.program_id(1)
    c = pl.program_id(2)
    L = chunk
    scale = ML_QK_DIM ** -0.5

    @pl.when(c == 0)
    def _():
        ct_ref[...] = jnp.zeros_like(ct_ref)
        n_ref[...] = jnp.zeros_like(n_ref)
        m_ref[...] = jnp.zeros_like(m_ref)

    gc = gcol_ref[0] + bcol_ref[...]
    gr = grow_ref[0] + brow_ref[...]
    lane = lax.broadcasted_iota(jnp.int32, gc.shape, 1)
    sub = lax.broadcasted_iota(jnp.int32, gr.shape, 0)
    i_col = jnp.sum(jnp.where(lane == h, gc, 0.0), axis=1, keepdims=True)
    f_col = jnp.sum(jnp.where(lane == heads + h, gc, 0.0), axis=1, keepdims=True)
    i_row = jnp.sum(jnp.where(sub == h, gr, 0.0), axis=0, keepdims=True)
    f_row = jnp.sum(jnp.where(sub == heads + h, gr, 0.0), axis=0, keepdims=True)
    lf_col = _log_sigmoid(f_col)
    lf_row = _log_sigmoid(f_row)

    tt = lax.broadcasted_iota(jnp.int32, (L, L), 0)
    ss = lax.broadcasted_iota(jnp.int32, (L, L), 1)
    causal = ss <= tt
    tri = jnp.where(causal, 1.0, 0.0).astype(BF16)
    tri_t = jnp.where(tt <= ss, 1.0, 0.0).astype(BF16)
    lf_col_b = jnp.broadcast_to(lf_col, (L, LANES))
    lf_row_b = jnp.broadcast_to(lf_row, (8, L))
    b_col = sum(jnp.dot(tri, part, preferred_element_type=F32) for part in _split3(lf_col_b))[:, 0:1]
    b_row = sum(jnp.dot(part, tri_t, preferred_element_type=F32) for part in _split3(lf_row_b))[0:1, :]
    b_last = b_col[L - 1:L, :]

    m_prev = m_ref[...]
    logw = jnp.where(causal, b_col - b_row + i_row, NEG)
    log_inter = b_col + m_prev
    m_t = jnp.maximum(log_inter, jnp.max(logw, axis=-1, keepdims=True))
    w_intra = jnp.exp(logw - m_t) * scale
    w_inter = jnp.exp(log_inter - m_t) * scale

    q = q_ref[0]
    k = k_ref[0]
    v = v_ref[0]
    ct = ct_ref[...]
    n_row = n_ref[...]
    a = lax.dot_general(q, k, (((1,), (1,)), ((), ())), preferred_element_type=F32) * w_intra
    num = (jnp.dot(a.astype(BF16), v, preferred_element_type=F32)
           + w_inter * jnp.dot(q, ct.astype(BF16), preferred_element_type=F32))
    den = (jnp.sum(a, axis=-1, keepdims=True)
           + w_inter * jnp.sum(q.astype(F32) * n_row, axis=-1, keepdims=True))
    hx = num / jnp.maximum(jnp.abs(den), jnp.exp(-m_t))

    m_new = m_t[L - 1:L, :]
    wk = jnp.exp(b_last - b_col + i_col - m_new)
    decay = jnp.exp(b_last + m_prev - m_new)
    kw = k.astype(F32) * wk
    ct_ref[...] = decay * ct + lax.dot_general(kw.astype(BF16), v, (((0,), (0,)), ((), ())),
                                               preferred_element_type=F32)
    n_ref[...] = decay * n_row + jnp.sum(kw, axis=0, keepdims=True)
    m_ref[...] = m_new

    ms = jnp.mean(hx * hx, axis=-1, keepdims=True)
    hn = hx * lax.rsqrt(ms + EPS) * ng_ref[...]
    out = hn * jax.nn.sigmoid(og_ref[0].astype(F32)) * _silu(sgate_ref[0].astype(F32))
    o_ref[0] = out.astype(o_ref.dtype)


def _mlstm(proj, gates, b_gates, norm_g, *, b, s, heads, chunk):
    dk, dv = ML_QK_DIM, ML_V_DIM
    L = _tile(s, chunk)
    ng = 2 * heads
    gates_t = jnp.swapaxes(gates, 1, 2)
    kern = functools.partial(_mlstm_kernel, chunk=L, heads=heads)
    return pl.pallas_call(
        kern,
        out_shape=jax.ShapeDtypeStruct((b, s, heads * dv), BF16),
        grid=(b, heads, s // L),
        in_specs=[
            pl.BlockSpec((1, L, ng), lambda bi, h, c: (bi, c, 0)),
            pl.BlockSpec((1, ng, L), lambda bi, h, c: (bi, 0, c)),
            pl.BlockSpec((1, ng), lambda bi, h, c: (0, 0)),
            pl.BlockSpec((ng, 1), lambda bi, h, c: (0, 0)),
            pl.BlockSpec((1, dv), lambda bi, h, c: (0, 0)),
            pl.BlockSpec((1, L, dk), lambda bi, h, c: (bi, c, h)),
            pl.BlockSpec((1, L, dk), lambda bi, h, c: (bi, c, heads + h)),
            pl.BlockSpec((1, L, dv), lambda bi, h, c: (bi, c, heads + h)),
            pl.BlockSpec((1, L, dv), lambda bi, h, c: (bi, c, 2 * heads + h)),
            pl.BlockSpec((1, L, dv), lambda bi, h, c: (bi, c, 3 * heads + h)),
        ],
        out_specs=pl.BlockSpec((1, L, dv), lambda bi, h, c: (bi, c, h)),
        scratch_shapes=[pltpu.VMEM((dk, dv), F32), pltpu.VMEM((1, dk), F32), pltpu.VMEM((1, 1), F32)],
        compiler_params=_cparams(("parallel", "parallel", "arbitrary")),
        name="mlstm",
    )(gates, gates_t, b_gates.reshape(1, ng).astype(F32), b_gates.reshape(ng, 1).astype(F32),
      norm_g.reshape(1, dv).astype(F32), proj, proj, proj, proj, proj)


def _rope_tables(positions, dim, q_scale):
    half = dim // 2
    inv_freq = ROPE_THETA ** (-jnp.arange(0, dim, 2, dtype=F32) / dim)
    ang = positions.astype(F32).reshape(-1)[:, None] * inv_freq
    cos, sin = jnp.cos(ang), jnp.sin(ang)
    zero = jnp.zeros_like(sin)
    reps = LANES // dim
    cos_f = jnp.tile(jnp.concatenate([cos, cos], axis=-1), (1, reps))
    if dim == LANES:
        sin_a = jnp.concatenate([-sin, sin], axis=-1)
        sin_b = jnp.zeros_like(sin_a)
    else:
        sin_a = jnp.tile(jnp.concatenate([-sin, zero], axis=-1), (1, reps))
        sin_b = jnp.tile(jnp.concatenate([zero, sin], axis=-1), (1, reps))
    stack = lambda t: jnp.stack([t * q_scale, t], axis=0)
    return stack(cos_f), stack(sin_a), stack(sin_b)


def kernel(x, positions, norm_g, final_g, da_w_in, da_w_out, da_lambda, da_subln_g, sw_w_in, sw_w_out,
           sw_sinks, ml_w_in, ml_b_gates, ml_w_out, ml_norm_g):
    b, s, d = x.shape
    depth = norm_g.shape[0]
    n = b * s
    tabs_a = _rope_tables(positions, DA_QK_DIM, DA_QK_DIM ** -0.5)
    tabs_b = _rope_tables(positions, SW_HEAD_DIM, SW_HEAD_DIM ** -0.5)
    da_heads = d // DA_V_DIM
    ml_heads = d // ML_V_DIM
    x2d = x.reshape(n, d).astype(F32)

    for i in range(depth):
        kind, j = i % N_MIXERS, i // N_MIXERS
        last = i == depth - 1
        if kind == 0:
            lam_init = 0.8 - 0.6 * math.exp(-0.3 * i)
            nq = 2 * da_heads * DA_QK_DIM
            tn = 1024
            proj = _norm_proj(x2d, norm_g[i], da_w_in[j].astype(BF16), tabs_a, tm=512, tn=tn,
                              n_rope_tiles=2 * nq // tn, nq_tiles=nq // tn, shift_a=64, shift_b=64)
            o = _diff_attention(proj.reshape(b, s, -1), da_lambda[j], da_subln_g[j], lam_init,
                                b=b, s=s, heads=da_heads, t=512)
            w_out = da_w_out[j]
        elif kind == 1:
            tn = 256
            nq = d
            proj = _norm_proj(x2d, norm_g[i], sw_w_in[j].astype(BF16), tabs_b, tm=512, tn=tn,
                              n_rope_tiles=(nq + 4 * SW_HEAD_DIM) // tn, nq_tiles=nq // tn,
                              shift_a=96, shift_b=32)
            o = _sliding_window_attention(proj.reshape(b, s, -1), sw_sinks[j], b=b, s=s)
            w_out = sw_w_out[j]
        else:
            w = ml_w_in[j]
            n_main = 2 * ml_heads * ML_QK_DIM + 2 * d
            ng = 2 * ml_heads
            w_main = jnp.concatenate([w[:, :n_main], w[:, n_main + ng:]], axis=1).astype(BF16)
            w_gate = jnp.pad(w[:, n_main:n_main + ng], ((0, 0), (0, LANES - ng))).astype(BF16)
            proj = _norm_proj(x2d, norm_g[i], w_main, tabs_a, tm=512, tn=1024)
            gates = _norm_proj(x2d, norm_g[i], w_gate, tabs_a, tm=512, tn=LANES, out_dtype=F32)[:, :ng]
            o = _mlstm(proj.reshape(b, s, -1), gates.reshape(b, s, ng), ml_b_gates[j], ml_norm_g[j],
                       b=b, s=s, heads=ml_heads, chunk=256)
            w_out = ml_w_out[j]
        x2d = _out_proj(o.reshape(n, d), w_out.astype(BF16), x2d, final_g, tm=512, final_norm=last)
    return x2d.reshape(b, s, d).astype(x.dtype)
```

```python
import functools
import math
from typing import NamedTuple

import jax
import jax.numpy as jnp
from jax import lax
from jax.experimental import pallas as pl
from jax.experimental.pallas import tpu as pltpu

F32 = jnp.float32
BF16 = jnp.bfloat16

ROPE_THETA = 10000.0
EPS = 1e-6
NEG = -1e30
N_MIXERS = 3

LANES = 128
MXU_N = 256
VMEM_LIMIT = 56 * 1024 * 1024

DA_QK_DIM = 128
DA_V_DIM = 256
SW_HEAD_DIM = 64
SW_GROUP = 8
SW_WINDOW = 128
ML_QK_DIM = 128
ML_V_DIM = 256
ONES_ROWS = 16
LOG2E = math.log2(math.e)

NT_DIMS = (((1,), (1,)), ((), ()))


def _cparams(sem):
    return pltpu.CompilerParams(dimension_semantics=sem, vmem_limit_bytes=VMEM_LIMIT)


def _tile(n, pref):
    t = min(n, pref)
    assert n % t == 0, (n, t)
    return t


class _ProjCfg(NamedTuple):
    tn: int
    n_tiles: int
    q_cols: int
    rope_cols: int
    half: int
    q_scale: float
    has_side: bool


def _rope_chunk(xc, cos, sin, half):
    if 2 * half == LANES:
        partner = pltpu.roll(xc, half, 1)
    else:
        lane = lax.broadcasted_iota(jnp.int32, xc.shape, 1)
        partner = jnp.where(lane % (2 * half) < half, pltpu.roll(xc, LANES - half, 1), pltpu.roll(xc, half, 1))
    return xc * cos + partner * sin


def _norm_proj_kernel(*refs, cfg):
    if cfg.has_side:
        x_ref, g_ref, w_ref, ws_ref, cos_ref, sin_ref, o_ref, side_ref, hn_ref = refs
    else:
        x_ref, g_ref, w_ref, cos_ref, sin_ref, o_ref, hn_ref = refs
    j = pl.program_id(1)

    @pl.when(j == 0)
    def _():
        x = x_ref[...]
        ms = jnp.mean(x * x, axis=-1, keepdims=True)
        hn = (x * lax.rsqrt(ms + EPS) * g_ref[...]).astype(BF16)
        hn_ref[...] = hn
        if cfg.has_side:
            side_ref[...] = jnp.dot(hn, ws_ref[...], preferred_element_type=F32)

    def chunk_mode(col):
        return "q" if col < cfg.q_cols else ("k" if col < cfg.rope_cols else None)

    def emit_tile(modes):
        hn = hn_ref[...]
        if any(modes):
            cos = cos_ref[...]
            sin = sin_ref[...]
        if "q" in modes:
            cos_q = cos * cfg.q_scale
            sin_q = sin * cfg.q_scale
        for cc in range(cfg.tn // MXU_N):
            acc = jnp.dot(hn, w_ref[:, cc * MXU_N:(cc + 1) * MXU_N], preferred_element_type=F32)
            for c in range(MXU_N // LANES):
                mode = modes[cc * (MXU_N // LANES) + c]
                xc = acc[:, c * LANES:(c + 1) * LANES]
                if mode == "q":
                    xc = _rope_chunk(xc, cos_q, sin_q, cfg.half)
                elif mode == "k":
                    xc = _rope_chunk(xc, cos, sin, cfg.half)
                lo = cc * MXU_N + c * LANES
                o_ref[:, lo:lo + LANES] = xc.astype(o_ref.dtype)

    tile_modes = [tuple(chunk_mode(jj * cfg.tn + c * LANES) for c in range(cfg.tn // LANES))
                  for jj in range(cfg.n_tiles)]
    start = 0
    for jj in range(1, cfg.n_tiles + 1):
        if jj == cfg.n_tiles or tile_modes[jj] != tile_modes[start]:
            pl.when((j >= start) & (j < jj))(functools.partial(emit_tile, tile_modes[start]))
            start = jj


def _norm_proj(x2d, g, w, cos, sin, *, tm, tn, q_cols=0, rope_cols=0, half=64, q_scale=1.0, w_side=None):
    n, d = x2d.shape
    dout = w.shape[1]
    tm = _tile(n, tm)
    assert dout % tn == 0 and tn % MXU_N == 0
    cfg = _ProjCfg(tn=tn, n_tiles=dout // tn, q_cols=q_cols, rope_cols=max(rope_cols, q_cols), half=half,
                   q_scale=q_scale, has_side=w_side is not None)
    in_specs = [
        pl.BlockSpec((tm, d), lambda i, j: (i, 0)),
        pl.BlockSpec((1, d), lambda i, j: (0, 0)),
        pl.BlockSpec((d, tn), lambda i, j: (0, j)),
    ]
    args = [x2d, g.reshape(1, d), w]
    out_shape = [jax.ShapeDtypeStruct((n, dout), BF16)]
    out_specs = [pl.BlockSpec((tm, tn), lambda i, j: (i, j))]
    if cfg.has_side:
        in_specs.append(pl.BlockSpec((d, LANES), lambda i, j: (0, 0)))
        args.append(w_side)
        out_shape.append(jax.ShapeDtypeStruct((n, LANES), F32))
        out_specs.append(pl.BlockSpec((tm, LANES), lambda i, j: (i, 0)))
    in_specs += [pl.BlockSpec((tm, LANES), lambda i, j: (i, 0))] * 2
    args += [cos, sin]
    res = pl.pallas_call(
        functools.partial(_norm_proj_kernel, cfg=cfg),
        out_shape=out_shape,
        grid=(n // tm, dout // tn),
        in_specs=in_specs,
        out_specs=out_specs,
        scratch_shapes=[pltpu.VMEM((tm, d), BF16)],
        compiler_params=_cparams(("parallel", "arbitrary")),
        name="norm_proj",
    )(*args)
    return res if cfg.has_side else res[0]


def _out_proj_kernel(o_ref, w_ref, x_ref, fg_ref, out_ref, *, final_norm):
    y = jnp.dot(o_ref[...], w_ref[...], preferred_element_type=F32)
    xn = x_ref[...] + y
    if final_norm:
        ms = jnp.mean(xn * xn, axis=-1, keepdims=True)
        xn = xn * lax.rsqrt(ms + EPS) * fg_ref[...]
    out_ref[...] = xn


def _out_proj(o2d, w, x2d, final_g, *, tm, final_norm):
    n, d = x2d.shape
    k = o2d.shape[1]
    tm = _tile(n, tm)
    return pl.pallas_call(
        functools.partial(_out_proj_kernel, final_norm=final_norm),
        out_shape=jax.ShapeDtypeStruct((n, d), F32),
        grid=(n // tm,),
        in_specs=[
            pl.BlockSpec((tm, k), lambda i: (i, 0)),
            pl.BlockSpec((k, d), lambda i: (0, 0)),
            pl.BlockSpec((tm, d), lambda i: (i, 0)),
            pl.BlockSpec((1, d), lambda i: (0, 0)),
        ],
        out_specs=pl.BlockSpec((tm, d), lambda i: (i, 0)),
        compiler_params=_cparams(("parallel",)),
        name="out_proj",
    )(o2d, w, x2d, final_g.reshape(1, d))


def _silu(g):
    return g * jax.nn.sigmoid(g)


def _da_kernel(lam_ref, sg_ref, q_ref, k_ref, v_ref, g_ref, o_ref, vt_ref, acc_ref, s_ref, p_ref, *, t, lam_init):
    i = pl.program_id(2)
    d = DA_QK_DIM
    dv = DA_V_DIM

    @pl.when(i == 0)
    def _():
        for c in range(vt_ref.shape[0]):
            vt_ref[c, :dv, :] = v_ref[0, c * t:(c + 1) * t, :].astype(F32).T.astype(BF16)
            vt_ref[c, dv:, :] = jnp.ones((ONES_ROWS, t), BF16)

    q = q_ref[0]
    qs = (q[:, :d], q[:, d:])
    acc_ref[...] = jnp.zeros_like(acc_ref)

    def qk(j, mp):
        off = pl.multiple_of(j * t, t)
        kb = k_ref[0, pl.ds(off, t), mp * d:(mp + 1) * d]
        s_ref[mp] = lax.dot_general(kb, qs[mp], NT_DIMS, preferred_element_type=F32)

    def softmax(mp, m, masked):
        if masked:
            key = lax.broadcasted_iota(jnp.int32, (t, t), 0)
            qry = lax.broadcasted_iota(jnp.int32, (t, t), 1)
            s_ref[mp] = jnp.where(key <= qry, s_ref[mp], NEG)
        m_new = jnp.maximum(m, jnp.max(s_ref[mp], axis=0, keepdims=True))
        p_ref[mp] = jnp.exp2(s_ref[mp] - m_new).astype(BF16)
        return m_new, jnp.exp2(m - m_new)

    def pv(j, mp, alpha):
        acc_ref[mp] = acc_ref[mp] * alpha + jnp.dot(vt_ref[j], p_ref[mp], preferred_element_type=F32)

    def step(j, ms, last):
        qk(j, 1)
        m1, a1 = softmax(0, ms[0], last)
        pv(j, 0, a1)
        if not last:
            qk(j + 1, 0)
        m2, a2 = softmax(1, ms[1], last)
        pv(j, 1, a2)
        return m1, m2

    qk(0, 0)
    m0 = jnp.full((1, t), NEG, F32)
    ms = lax.fori_loop(0, i, functools.partial(step, last=False), (m0, m0))
    step(i, ms, True)

    lam = lam_ref[...]
    lam_full = (jnp.exp(jnp.sum(lam[0:1] * lam[1:2], axis=-1, keepdims=True))
                - jnp.exp(jnp.sum(lam[2:3] * lam[3:4], axis=-1, keepdims=True)) + lam_init)
    inv1 = 1.0 / acc_ref[0, dv:dv + 1, :]
    inv2 = 1.0 / acc_ref[1, dv:dv + 1, :]
    o_t = acc_ref[0, :dv, :] * inv1 - lam_full * (acc_ref[1, :dv, :] * inv2)
    o = o_t.T
    ms = jnp.mean(o * o, axis=-1, keepdims=True)
    o = o * lax.rsqrt(ms + EPS) * sg_ref[...] * (1.0 - lam_init)
    o = o * _silu(g_ref[0].astype(F32))
    o_ref[0] = o.astype(o_ref.dtype)


def _diff_attention(proj, lam, subln_g, lam_init, *, b, s, heads, t):
    dv = DA_V_DIM
    t = _tile(s, t)
    kern = functools.partial(_da_kernel, t=t, lam_init=lam_init)
    return pl.pallas_call(
        kern,
        out_shape=jax.ShapeDtypeStruct((b, s, heads * dv), BF16),
        grid=(b, heads, s // t),
        in_specs=[
            pl.BlockSpec((4, DA_QK_DIM), lambda bi, h, i: (0, 0)),
            pl.BlockSpec((1, dv), lambda bi, h, i: (0, 0)),
            pl.BlockSpec((1, t, dv), lambda bi, h, i: (bi, i, h)),
            pl.BlockSpec((1, s, dv), lambda bi, h, i: (bi, 0, heads + h)),
            pl.BlockSpec((1, s, dv), lambda bi, h, i: (bi, 0, 2 * heads + h)),
            pl.BlockSpec((1, t, dv), lambda bi, h, i: (bi, i, 3 * heads + h)),
        ],
        out_specs=pl.BlockSpec((1, t, dv), lambda bi, h, i: (bi, i, h)),
        scratch_shapes=[pltpu.VMEM((s // t, dv + ONES_ROWS, t), BF16), pltpu.VMEM((2, dv + ONES_ROWS, t), F32),
                        pltpu.VMEM((2, t, t), F32), pltpu.VMEM((2, t, t), BF16)],
        compiler_params=_cparams(("parallel", "parallel", "arbitrary")),
        name="diff_attention",
    )(lam.astype(F32), subln_g.reshape(1, dv).astype(F32), proj, proj, proj, proj)


def _swa_kernel(sink_ref, q_ref, kvc_ref, kvp_ref, g0_ref, g1_ref, g2_ref, g3_ref, o_ref):
    i = pl.program_id(1)
    w = SW_WINDOW
    hd = SW_HEAD_DIM
    nkv = 4
    g_refs = (g0_ref, g1_ref, g2_ref, g3_ref)
    has_prev = i > 0

    lane = lax.broadcasted_iota(jnp.int32, (w, LANES), 1)
    lo = lane < hd
    rows = 4 * w
    row = lax.broadcasted_iota(jnp.int32, (rows, w), 0) % w
    col = lax.broadcasted_iota(jnp.int32, (rows, w), 1)
    lower = col <= row

    def split_pair(x_bf16, parity):
        xf = x_bf16.astype(F32)
        if parity == 0:
            a = jnp.where(lo, xf, 0.0)
            bb = pltpu.roll(a, hd, 1)
        else:
            bb = jnp.where(lo, 0.0, xf)
            a = pltpu.roll(bb, hd, 1)
        return a.astype(BF16), bb.astype(BF16)

    kvc = kvc_ref[0]
    kvp = kvp_ref[0]
    q_all = q_ref[0]
    for kvh in range(nkv):
        c, parity = kvh // 2, kvh % 2
        kc = split_pair(kvc[:, c * LANES:(c + 1) * LANES], parity)
        kp = split_pair(kvp[:, c * LANES:(c + 1) * LANES], parity)
        vc = split_pair(kvc[:, (2 + c) * LANES:(3 + c) * LANES], parity)
        vp = split_pair(kvp[:, (2 + c) * LANES:(3 + c) * LANES], parity)
        base = kvh * SW_GROUP * hd
        q4 = jnp.concatenate([q_all[:, base + jj * LANES: base + (jj + 1) * LANES] for jj in range(4)], axis=0)
        o4 = jnp.zeros((rows, LANES), F32)
        for ab in range(2):
            s_cur = lax.dot_general(q4, kc[ab], NT_DIMS, preferred_element_type=F32)
            s_prev = lax.dot_general(q4, kp[ab], NT_DIMS, preferred_element_type=F32)
            s_prev = jnp.where(has_prev, s_prev, NEG)
            s = jnp.where(lower, s_cur, s_prev)
            sink = jnp.concatenate(
                [jnp.full((w, 1), sink_ref[kvh * SW_GROUP + 2 * jj + ab], F32) for jj in range(4)], axis=0)
            m = jnp.maximum(jnp.max(s, axis=-1, keepdims=True), sink)
            e = jnp.exp(s - m)
            denom = jnp.sum(e, axis=-1, keepdims=True) + jnp.exp(sink - m)
            p = e * (1.0 / denom)
            p_cur = jnp.where(lower, p, 0.0).astype(BF16)
            p_prev = jnp.where(lower, 0.0, p).astype(BF16)
            o4 = o4 + jnp.dot(p_cur, vc[ab], preferred_element_type=F32)
            o4 = o4 + jnp.dot(p_prev, vp[ab], preferred_element_type=F32)
        gate = g_refs[kvh][0].astype(F32)
        for jj in range(4):
            og = o4[jj * w:(jj + 1) * w] * _silu(gate[:, jj * LANES:(jj + 1) * LANES])
            o_ref[0, :, base + jj * LANES: base + (jj + 1) * LANES] = og.astype(o_ref.dtype)


def _sliding_window_attention(proj, sinks, *, b, s):
    w = SW_WINDOW
    width = 32 * SW_HEAD_DIM
    gw = SW_GROUP * SW_HEAD_DIM
    kv_blk = width // gw
    g_specs = [pl.BlockSpec((1, w, gw), functools.partial(lambda bi, i, kk: (bi, i, kk), kk=kv_blk + 1 + kvh))
               for kvh in range(4)]
    return pl.pallas_call(
        _swa_kernel,
        out_shape=jax.ShapeDtypeStruct((b, s, width), BF16),
        grid=(b, s // w),
        in_specs=[
            pl.BlockSpec(memory_space=pltpu.SMEM),
            pl.BlockSpec((1, w, width), lambda bi, i: (bi, i, 0)),
            pl.BlockSpec((1, w, gw), lambda bi, i: (bi, i, kv_blk)),
            pl.BlockSpec((1, w, gw), lambda bi, i: (bi, jnp.maximum(i - 1, 0), kv_blk)),
        ] + g_specs,
        out_specs=pl.BlockSpec((1, w, width), lambda bi, i: (bi, i, 0)),
        compiler_params=_cparams(("parallel", "arbitrary")),
        name="sliding_window_attention",
    )(sinks.astype(F32), proj, proj, proj, proj, proj, proj, proj)


def _log_sigmoid(x):
    return jnp.minimum(x, 0.0) - jnp.log1p(jnp.exp(-jnp.abs(x)))


def _split3(x):
    hi = x.astype(BF16)
    r1 = x - hi.astype(F32)
    mid = r1.astype(BF16)
    lo = (r1 - mid.astype(F32)).astype(BF16)
    return hi, mid, lo


def _mlstm_kernel(gcol_ref, grow_ref, bcol_ref, brow_ref, ng_ref, q_ref, k_ref, v_ref, og_ref, sgate_ref,
                  o_ref, ct_ref, n_ref, m_ref, *, chunk, heads):
    h = pl.program_id(1)
    c = pl.program_id(2)
    L = chunk
    scale = ML_QK_DIM ** -0.5

    @pl.when(c == 0)
    def _():
        ct_ref[...] = jnp.zeros_like(ct_ref)
        n_ref[...] = jnp.zeros_like(n_ref)
        m_ref[...] = jnp.zeros_like(m_ref)

    gc = gcol_ref[0] + bcol_ref[...]
    gr = grow_ref[0] + brow_ref[...]
    lane = lax.broadcasted_iota(jnp.int32, gc.shape, 1)
    sub = lax.broadcasted_iota(jnp.int32, gr.shape, 0)
    i_col = jnp.sum(jnp.where(lane == h, gc, 0.0), axis=1, keepdims=True)
    f_col = jnp.sum(jnp.where(lane == heads + h, gc, 0.0), axis=1, keepdims=True)
    i_row = jnp.sum(jnp.where(sub == h, gr, 0.0), axis=0, keepdims=True)
    f_row = jnp.sum(jnp.where(sub == heads + h, gr, 0.0), axis=0, keepdims=True)
    lf_col = _log_sigmoid(f_col)
    lf_row = _log_sigmoid(f_row)

    tt = lax.broadcasted_iota(jnp.int32, (L, L), 0)
    ss = lax.broadcasted_iota(jnp.int32, (L, L), 1)
    causal = ss <= tt
    tri = jnp.where(causal, 1.0, 0.0).astype(BF16)
    tri_t = jnp.where(tt <= ss, 1.0, 0.0).astype(BF16)
    lf_col_b = jnp.broadcast_to(lf_col, (L, LANES))
    lf_row_b = jnp.broadcast_to(lf_row, (8, L))
    b_col = sum(jnp.dot(tri, part, preferred_element_type=F32) for part in _split3(lf_col_b))[:, 0:1]
    b_row = sum(jnp.dot(part, tri_t, preferred_element_type=F32) for part in _split3(lf_row_b))[0:1, :]
    b_last = b_col[L - 1:L, :]

    m_prev = m_ref[...]
    logw = jnp.where(causal, b_col - b_row + i_row, NEG)
    log_inter = b_col + m_prev
    m_t = jnp.maximum(log_inter, jnp.max(logw, axis=-1, keepdims=True))
    w_intra = jnp.exp(logw - m_t) * scale
    w_inter = jnp.exp(log_inter - m_t) * scale

    q = q_ref[0]
    k = k_ref[0]
    v = v_ref[0]
    ct = ct_ref[...]
    n_row = n_ref[...]
    a = lax.dot_general(q, k, NT_DIMS, preferred_element_type=F32) * w_intra
    num = (jnp.dot(a.astype(BF16), v, preferred_element_type=F32)
           + w_inter * jnp.dot(q, ct.astype(BF16), preferred_element_type=F32))
    den = (jnp.sum(a, axis=-1, keepdims=True)
           + w_inter * jnp.sum(q.astype(F32) * n_row, axis=-1, keepdims=True))
    hx = num / jnp.maximum(jnp.abs(den), jnp.exp(-m_t))

    m_new = m_t[L - 1:L, :]
    wk = jnp.exp(b_last - b_col + i_col - m_new)
    decay = jnp.exp(b_last + m_prev - m_new)
    kw = k.astype(F32) * wk
    ct_ref[...] = decay * ct + lax.dot_general(kw.astype(BF16), v, (((0,), (0,)), ((), ())),
                                               preferred_element_type=F32)
    n_ref[...] = decay * n_row + jnp.sum(kw, axis=0, keepdims=True)
    m_ref[...] = m_new

    ms = jnp.mean(hx * hx, axis=-1, keepdims=True)
    hn = hx * lax.rsqrt(ms + EPS) * ng_ref[...]
    out = hn * jax.nn.sigmoid(og_ref[0].astype(F32)) * _silu(sgate_ref[0].astype(F32))
    o_ref[0] = out.astype(o_ref.dtype)


def _mlstm(proj, gates, b_gates, norm_g, *, b, s, heads, chunk):
    dk, dv = ML_QK_DIM, ML_V_DIM
    L = _tile(s, chunk)
    ng = 2 * heads
    gates_t = jnp.swapaxes(gates, 1, 2)
    kern = functools.partial(_mlstm_kernel, chunk=L, heads=heads)
    return pl.pallas_call(
        kern,
        out_shape=jax.ShapeDtypeStruct((b, s, heads * dv), BF16),
        grid=(b, heads, s // L),
        in_specs=[
            pl.BlockSpec((1, L, ng), lambda bi, h, c: (bi, c, 0)),
            pl.BlockSpec((1, ng, L), lambda bi, h, c: (bi, 0, c)),
            pl.BlockSpec((1, ng), lambda bi, h, c: (0, 0)),
            pl.BlockSpec((ng, 1), lambda bi, h, c: (0, 0)),
            pl.BlockSpec((1, dv), lambda bi, h, c: (0, 0)),
            pl.BlockSpec((1, L, dk), lambda bi, h, c: (bi, c, h)),
            pl.BlockSpec((1, L, dk), lambda bi, h, c: (bi, c, heads + h)),
            pl.BlockSpec((1, L, dv), lambda bi, h, c: (bi, c, heads + h)),
            pl.BlockSpec((1, L, dv), lambda bi, h, c: (bi, c, 2 * heads + h)),
            pl.BlockSpec((1, L, dv), lambda bi, h, c: (bi, c, 3 * heads + h)),
        ],
        out_specs=pl.BlockSpec((1, L, dv), lambda bi, h, c: (bi, c, h)),
        scratch_shapes=[pltpu.VMEM((dk, dv), F32), pltpu.VMEM((1, dk), F32), pltpu.VMEM((1, 1), F32)],
        compiler_params=_cparams(("parallel", "parallel", "arbitrary")),
        name="mlstm",
    )(gates, gates_t, b_gates.reshape(1, ng).astype(F32), b_gates.reshape(ng, 1).astype(F32),
      norm_g.reshape(1, dv).astype(F32), proj, proj, proj, proj, proj)


def _rope_tables(positions, dim):
    inv_freq = ROPE_THETA ** (-jnp.arange(0, dim, 2, dtype=F32) / dim)
    ang = positions.astype(F32).reshape(-1)[:, None] * inv_freq
    cos, sin = jnp.cos(ang), jnp.sin(ang)
    reps = LANES // dim
    return (jnp.tile(jnp.concatenate([cos, cos], axis=-1), (1, reps)),
            jnp.tile(jnp.concatenate([-sin, sin], axis=-1), (1, reps)))


def kernel(x, positions, norm_g, final_g, da_w_in, da_w_out, da_lambda, da_subln_g, sw_w_in, sw_w_out,
           sw_sinks, ml_w_in, ml_b_gates, ml_w_out, ml_norm_g):
    b, s, d = x.shape
    depth = norm_g.shape[0]
    n = b * s
    cos_a, sin_a = _rope_tables(positions, DA_QK_DIM)
    cos_b, sin_b = _rope_tables(positions, SW_HEAD_DIM)
    da_heads = d // DA_V_DIM
    ml_heads = d // ML_V_DIM
    x2d = x.reshape(n, d).astype(F32)
    tm = 1024

    for i in range(depth):
        kind, j = i % N_MIXERS, i // N_MIXERS
        last = i == depth - 1
        if kind == 0:
            lam_init = 0.8 - 0.6 * math.exp(-0.3 * i)
            nq = 2 * da_heads * DA_QK_DIM
            proj = _norm_proj(x2d, norm_g[i], da_w_in[j].astype(BF16), cos_a, sin_a, tm=tm, tn=1024,
                              q_cols=nq, rope_cols=2 * nq, half=DA_QK_DIM // 2, q_scale=LOG2E * DA_QK_DIM ** -0.5)
            o = _diff_attention(proj.reshape(b, s, -1), da_lambda[j], da_subln_g[j], lam_init,
                                b=b, s=s, heads=da_heads, t=512)
            w_out = da_w_out[j]
        elif kind == 1:
            proj = _norm_proj(x2d, norm_g[i], sw_w_in[j].astype(BF16), cos_b, sin_b, tm=tm, tn=512,
                              q_cols=d, rope_cols=d + 4 * SW_HEAD_DIM, half=SW_HEAD_DIM // 2,
                              q_scale=SW_HEAD_DIM ** -0.5)
            o = _sliding_window_attention(proj.reshape(b, s, -1), sw_sinks[j], b=b, s=s)
            w_out = sw_w_out[j]
        else:
            w = ml_w_in[j]
            n_main = 2 * ml_heads * ML_QK_DIM + 2 * d
            ng = 2 * ml_heads
            w_main = jnp.concatenate([w[:, :n_main], w[:, n_main + ng:]], axis=1).astype(BF16)
            w_gate = jnp.pad(w[:, n_main:n_main + ng], ((0, 0), (0, LANES - ng))).astype(BF16)
            proj, gates = _norm_proj(x2d, norm_g[i], w_main, cos_a, sin_a, tm=tm, tn=1024, w_side=w_gate)
            o = _mlstm(proj.reshape(b, s, -1), gates[:, :ng].reshape(b, s, ng), ml_b_gates[j], ml_norm_g[j],
                       b=b, s=s, heads=ml_heads, chunk=256)
            w_out = ml_w_out[j]
        x2d = _out_proj(o.reshape(n, d), w_out.astype(BF16), x2d, final_g, tm=512, final_norm=last)
    return x2d.reshape(b, s, d).astype(x.dtype)
```

```python
import functools
import math
from typing import NamedTuple

import jax
import jax.numpy as jnp
from jax import lax
from jax.experimental import pallas as pl
from jax.experimental.pallas import tpu as pltpu

F32 = jnp.float32
BF16 = jnp.bfloat16

ROPE_THETA = 10000.0
EPS = 1e-6
NEG = -1e30
N_MIXERS = 3

LANES = 128
MXU_N = 256
VMEM_LIMIT = 56 * 1024 * 1024

DA_QK_DIM = 128
DA_V_DIM = 256
SW_HEAD_DIM = 64
SW_GROUP = 8
SW_WINDOW = 128
ML_QK_DIM = 128
ML_V_DIM = 256
ONES_ROWS = 16
LOG2E = math.log2(math.e)

NT_DIMS = (((1,), (1,)), ((), ()))


def _cparams(sem, flags=None):
    return pltpu.CompilerParams(dimension_semantics=sem, vmem_limit_bytes=VMEM_LIMIT, flags=flags)


def _tile(n, pref):
    t = min(n, pref)
    assert n % t == 0, (n, t)
    return t


class _ProjCfg(NamedTuple):
    tn: int
    n_tiles: int
    q_cols: int
    rope_cols: int
    half: int
    q_scale: float
    has_side: bool


def _rope_chunk(xc, cos, sin, half):
    if 2 * half == LANES:
        partner = pltpu.roll(xc, half, 1)
    else:
        lane = lax.broadcasted_iota(jnp.int32, xc.shape, 1)
        partner = jnp.where(lane % (2 * half) < half, pltpu.roll(xc, LANES - half, 1), pltpu.roll(xc, half, 1))
    return xc * cos + partner * sin


def _norm_proj_kernel(*refs, cfg):
    if cfg.has_side:
        x_ref, g_ref, w_ref, ws_ref, cos_ref, sin_ref, o_ref, side_ref, hn_ref = refs
    else:
        x_ref, g_ref, w_ref, cos_ref, sin_ref, o_ref, hn_ref = refs
    j = pl.program_id(1)

    @pl.when(j == 0)
    def _():
        x = x_ref[...]
        ms = jnp.mean(x * x, axis=-1, keepdims=True)
        hn = (x * lax.rsqrt(ms + EPS) * g_ref[...]).astype(BF16)
        hn_ref[...] = hn
        if cfg.has_side:
            side_ref[...] = jnp.dot(hn, ws_ref[...], preferred_element_type=F32)

    def chunk_mode(col):
        return "q" if col < cfg.q_cols else ("k" if col < cfg.rope_cols else None)

    def emit_tile(modes):
        hn = hn_ref[...]
        if any(modes):
            cos = cos_ref[...]
            sin = sin_ref[...]
        if "q" in modes:
            cos_q = cos * cfg.q_scale
            sin_q = sin * cfg.q_scale
        for cc in range(cfg.tn // MXU_N):
            acc = jnp.dot(hn, w_ref[:, cc * MXU_N:(cc + 1) * MXU_N], preferred_element_type=F32)
            for c in range(MXU_N // LANES):
                mode = modes[cc * (MXU_N // LANES) + c]
                xc = acc[:, c * LANES:(c + 1) * LANES]
                if mode == "q":
                    xc = _rope_chunk(xc, cos_q, sin_q, cfg.half)
                elif mode == "k":
                    xc = _rope_chunk(xc, cos, sin, cfg.half)
                lo = cc * MXU_N + c * LANES
                o_ref[:, lo:lo + LANES] = xc.astype(o_ref.dtype)

    tile_modes = [tuple(chunk_mode(jj * cfg.tn + c * LANES) for c in range(cfg.tn // LANES))
                  for jj in range(cfg.n_tiles)]
    start = 0
    for jj in range(1, cfg.n_tiles + 1):
        if jj == cfg.n_tiles or tile_modes[jj] != tile_modes[start]:
            pl.when((j >= start) & (j < jj))(functools.partial(emit_tile, tile_modes[start]))
            start = jj


def _norm_proj(x2d, g, w, cos, sin, *, tm, tn, q_cols=0, rope_cols=0, half=64, q_scale=1.0, w_side=None):
    n, d = x2d.shape
    dout = w.shape[1]
    tm = _tile(n, tm)
    assert dout % tn == 0 and tn % MXU_N == 0
    cfg = _ProjCfg(tn=tn, n_tiles=dout // tn, q_cols=q_cols, rope_cols=max(rope_cols, q_cols), half=half,
                   q_scale=q_scale, has_side=w_side is not None)
    in_specs = [
        pl.BlockSpec((tm, d), lambda i, j: (i, 0)),
        pl.BlockSpec((1, d), lambda i, j: (0, 0)),
        pl.BlockSpec((d, tn), lambda i, j: (0, j)),
    ]
    args = [x2d, g.reshape(1, d), w]
    out_shape = [jax.ShapeDtypeStruct((n, dout), BF16)]
    out_specs = [pl.BlockSpec((tm, tn), lambda i, j: (i, j))]
    if cfg.has_side:
        in_specs.append(pl.BlockSpec((d, LANES), lambda i, j: (0, 0)))
        args.append(w_side)
        out_shape.append(jax.ShapeDtypeStruct((n, LANES), F32))
        out_specs.append(pl.BlockSpec((tm, LANES), lambda i, j: (i, 0)))
    in_specs += [pl.BlockSpec((tm, LANES), lambda i, j: (i, 0))] * 2
    args += [cos, sin]
    res = pl.pallas_call(
        functools.partial(_norm_proj_kernel, cfg=cfg),
        out_shape=out_shape,
        grid=(n // tm, dout // tn),
        in_specs=in_specs,
        out_specs=out_specs,
        scratch_shapes=[pltpu.VMEM((tm, d), BF16)],
        compiler_params=_cparams(("parallel", "arbitrary")),
        name="norm_proj",
    )(*args)
    return res if cfg.has_side else res[0]


def _out_proj_kernel(o_ref, w_ref, x_ref, fg_ref, out_ref, *, final_norm):
    y = jnp.dot(o_ref[...], w_ref[...], preferred_element_type=F32)
    xn = x_ref[...] + y
    if final_norm:
        ms = jnp.mean(xn * xn, axis=-1, keepdims=True)
        xn = xn * lax.rsqrt(ms + EPS) * fg_ref[...]
    out_ref[...] = xn


def _out_proj(o2d, w, x2d, final_g, *, tm, final_norm):
    n, d = x2d.shape
    k = o2d.shape[1]
    tm = _tile(n, tm)
    return pl.pallas_call(
        functools.partial(_out_proj_kernel, final_norm=final_norm),
        out_shape=jax.ShapeDtypeStruct((n, d), F32),
        grid=(n // tm,),
        in_specs=[
            pl.BlockSpec((tm, k), lambda i: (i, 0)),
            pl.BlockSpec((k, d), lambda i: (0, 0)),
            pl.BlockSpec((tm, d), lambda i: (i, 0)),
            pl.BlockSpec((1, d), lambda i: (0, 0)),
        ],
        out_specs=pl.BlockSpec((tm, d), lambda i: (i, 0)),
        compiler_params=_cparams(("parallel",)),
        name="out_proj",
    )(o2d, w, x2d, final_g.reshape(1, d))


def _silu(g):
    return g * jax.nn.sigmoid(g)


def _da_kernel(lam_ref, sg_ref, q_ref, k_ref, v_ref, g_ref, o_ref, vt_ref, acc_ref, s_ref, p_ref, *,
               tq, tk, lam_init):
    i = pl.program_id(2)
    d = DA_QK_DIM
    dv = DA_V_DIM

    @pl.when(i == 0)
    def _():
        for c in range(vt_ref.shape[0]):
            vt_ref[c, :dv, :] = v_ref[0, c * tk:(c + 1) * tk, :].astype(F32).T.astype(BF16)
            vt_ref[c, dv:, :] = jnp.ones((ONES_ROWS, tk), BF16)

    q = q_ref[0]
    qs = (q[:, :d], q[:, d:])
    acc_ref[...] = jnp.zeros_like(acc_ref)

    def qk(j, buf):
        off = pl.multiple_of(j * tk, tk)
        for mp in range(2):
            kb = k_ref[0, pl.ds(off, tk), mp * d:(mp + 1) * d]
            s_ref[buf, mp] = lax.dot_general(kb, qs[mp], NT_DIMS, preferred_element_type=F32)

    def softmax(j, buf, mp, m, masked):
        if masked:
            key = j * tk + lax.broadcasted_iota(jnp.int32, (tk, tq), 0)
            qry = i * tq + lax.broadcasted_iota(jnp.int32, (tk, tq), 1)
            s_ref[buf, mp] = jnp.where(key <= qry, s_ref[buf, mp], NEG)
        m_new = jnp.maximum(m, jnp.max(s_ref[buf, mp], axis=0, keepdims=True))
        p_ref[buf, mp] = jnp.exp2(s_ref[buf, mp] - m_new).astype(BF16)
        return m_new, jnp.exp2(m - m_new)

    def pv(j, buf, mp, alpha):
        acc_ref[mp] = acc_ref[mp] * alpha + jnp.dot(vt_ref[j], p_ref[buf, mp], preferred_element_type=F32)

    def step(j, ms, buf, last):
        if not last:
            qk(j + 1, 1 - buf)
        new = []
        for mp in range(2):
            m_new, alpha = softmax(j, buf, mp, ms[mp], last)
            pv(j, buf, mp, alpha)
            new.append(m_new)
        return tuple(new)

    def any_step(j, ms):
        return lax.cond(j % 2 == 0, lambda ms: step(j, ms, 0, False), lambda ms: step(j, ms, 1, False), ms)

    n_full = (i * tq) // tk
    qk(0, 0)
    m0 = jnp.full((1, tq), NEG, F32)
    ms = lax.fori_loop(0, n_full, any_step, (m0, m0))
    for parity in range(2):
        @pl.when(n_full % 2 == parity)
        def _(parity=parity):
            step(n_full, ms, parity, True)

    lam = lam_ref[...]
    lam_full = (jnp.exp(jnp.sum(lam[0:1] * lam[1:2], axis=-1, keepdims=True))
                - jnp.exp(jnp.sum(lam[2:3] * lam[3:4], axis=-1, keepdims=True)) + lam_init)
    inv1 = 1.0 / acc_ref[0, dv:dv + 1, :]
    inv2 = 1.0 / acc_ref[1, dv:dv + 1, :]
    o_t = acc_ref[0, :dv, :] * inv1 - lam_full * (acc_ref[1, :dv, :] * inv2)
    o = o_t.T
    ms = jnp.mean(o * o, axis=-1, keepdims=True)
    o = o * lax.rsqrt(ms + EPS) * sg_ref[...] * (1.0 - lam_init)
    o = o * _silu(g_ref[0].astype(F32))
    o_ref[0] = o.astype(o_ref.dtype)


def _diff_attention(proj, lam, subln_g, lam_init, *, b, s, heads, tq, tk):
    dv = DA_V_DIM
    tq, tk = _tile(s, tq), _tile(s, tk)
    assert tk % tq == 0
    kern = functools.partial(_da_kernel, tq=tq, tk=tk, lam_init=lam_init)
    return pl.pallas_call(
        kern,
        out_shape=jax.ShapeDtypeStruct((b, s, heads * dv), BF16),
        grid=(b, heads, s // tq),
        in_specs=[
            pl.BlockSpec((4, DA_QK_DIM), lambda bi, h, i: (0, 0)),
            pl.BlockSpec((1, dv), lambda bi, h, i: (0, 0)),
            pl.BlockSpec((1, tq, dv), lambda bi, h, i: (bi, i, h)),
            pl.BlockSpec((1, s, dv), lambda bi, h, i: (bi, 0, heads + h)),
            pl.BlockSpec((1, s, dv), lambda bi, h, i: (bi, 0, 2 * heads + h)),
            pl.BlockSpec((1, tq, dv), lambda bi, h, i: (bi, i, 3 * heads + h)),
        ],
        out_specs=pl.BlockSpec((1, tq, dv), lambda bi, h, i: (bi, i, h)),
        scratch_shapes=[pltpu.VMEM((s // tk, dv + ONES_ROWS, tk), BF16), pltpu.VMEM((2, dv + ONES_ROWS, tq), F32),
                        pltpu.VMEM((2, 2, tk, tq), F32), pltpu.VMEM((2, 2, tk, tq), BF16)],
        compiler_params=_cparams(("parallel", "parallel", "arbitrary")),
        name="diff_attention",
    )(lam.astype(F32), subln_g.reshape(1, dv).astype(F32), proj, proj, proj, proj)


def _swa_kernel(sink_ref, q_ref, kvc_ref, kvp_ref, g0_ref, g1_ref, g2_ref, g3_ref, o_ref):
    i = pl.program_id(1)
    w = SW_WINDOW
    hd = SW_HEAD_DIM
    nkv = 4
    g_refs = (g0_ref, g1_ref, g2_ref, g3_ref)
    has_prev = i > 0

    lane = lax.broadcasted_iota(jnp.int32, (w, LANES), 1)
    lo = lane < hd
    rows = 4 * w
    row = lax.broadcasted_iota(jnp.int32, (rows, w), 0) % w
    col = lax.broadcasted_iota(jnp.int32, (rows, w), 1)
    lower = col <= row
    ones_w = jnp.ones((w, LANES), BF16)

    def split_pair(x_bf16, parity):
        xf = x_bf16.astype(F32)
        if parity == 0:
            a = jnp.where(lo, xf, 0.0)
            bb = pltpu.roll(a, hd, 1)
        else:
            bb = jnp.where(lo, 0.0, xf)
            a = pltpu.roll(bb, hd, 1)
        return a.astype(BF16), bb.astype(BF16)

    kvc = kvc_ref[0]
    kvp = kvp_ref[0]
    q_all = q_ref[0]
    for kvh in range(nkv):
        c, parity = kvh // 2, kvh % 2
        kc = split_pair(kvc[:, c * LANES:(c + 1) * LANES], parity)
        kp = split_pair(kvp[:, c * LANES:(c + 1) * LANES], parity)
        vc = split_pair(kvc[:, (2 + c) * LANES:(3 + c) * LANES], parity)
        vp = split_pair(kvp[:, (2 + c) * LANES:(3 + c) * LANES], parity)
        base = kvh * SW_GROUP * hd
        q4 = jnp.concatenate([q_all[:, base + jj * LANES: base + (jj + 1) * LANES] for jj in range(4)], axis=0)
        o4 = jnp.zeros((rows, LANES), F32)
        for ab in range(2):
            k2 = jnp.concatenate([kp[ab], kc[ab]], axis=0)
            v2 = jnp.concatenate([vp[ab], vc[ab]], axis=0)
            s2 = lax.dot_general(q4, k2, NT_DIMS, preferred_element_type=F32)
            s = jnp.where(lower, s2[:, w:], jnp.where(has_prev, s2[:, :w], NEG))
            sink = jnp.concatenate(
                [jnp.full((w, LANES), sink_ref[kvh * SW_GROUP + 2 * jj + ab], F32) for jj in range(4)], axis=0)
            m = jnp.maximum(jnp.max(s, axis=-1, keepdims=True), sink)
            e = jnp.exp(s - m)
            p2 = jnp.concatenate([jnp.where(lower, 0.0, e), jnp.where(lower, e, 0.0)], axis=1).astype(BF16)
            denom = jnp.dot(e.astype(BF16), ones_w, preferred_element_type=F32) + jnp.exp(sink - m)
            o4 = o4 + jnp.dot(p2, v2, preferred_element_type=F32) * (1.0 / denom)
        gate = g_refs[kvh][0].astype(F32)
        for jj in range(4):
            og = o4[jj * w:(jj + 1) * w] * _silu(gate[:, jj * LANES:(jj + 1) * LANES])
            o_ref[0, :, base + jj * LANES: base + (jj + 1) * LANES] = og.astype(o_ref.dtype)


def _sliding_window_attention(proj, sinks, *, b, s):
    w = SW_WINDOW
    width = 32 * SW_HEAD_DIM
    gw = SW_GROUP * SW_HEAD_DIM
    kv_blk = width // gw
    g_specs = [pl.BlockSpec((1, w, gw), functools.partial(lambda bi, i, kk: (bi, i, kk), kk=kv_blk + 1 + kvh))
               for kvh in range(4)]
    return pl.pallas_call(
        _swa_kernel,
        out_shape=jax.ShapeDtypeStruct((b, s, width), BF16),
        grid=(b, s // w),
        in_specs=[
            pl.BlockSpec(memory_space=pltpu.SMEM),
            pl.BlockSpec((1, w, width), lambda bi, i: (bi, i, 0)),
            pl.BlockSpec((1, w, gw), lambda bi, i: (bi, i, kv_blk)),
            pl.BlockSpec((1, w, gw), lambda bi, i: (bi, jnp.maximum(i - 1, 0), kv_blk)),
        ] + g_specs,
        out_specs=pl.BlockSpec((1, w, width), lambda bi, i: (bi, i, 0)),
        compiler_params=_cparams(("parallel", "arbitrary")),
        name="sliding_window_attention",
    )(sinks.astype(F32), proj, proj, proj, proj, proj, proj, proj)


def _log_sigmoid(x):
    return jnp.minimum(x, 0.0) - jnp.log1p(jnp.exp(-jnp.abs(x)))


def _split3(x):
    hi = x.astype(BF16)
    r1 = x - hi.astype(F32)
    mid = r1.astype(BF16)
    lo = (r1 - mid.astype(F32)).astype(BF16)
    return hi, mid, lo


def _mlstm_kernel(gcol_ref, grow_ref, bcol_ref, brow_ref, ng_ref, q_ref, k_ref, v_ref, og_ref, sgate_ref,
                  o_ref, ct_ref, n_ref, m_ref, *, chunk, heads, group):
    hp = pl.program_id(1)
    c = pl.program_id(2)
    L = chunk
    dk, dv = ML_QK_DIM, ML_V_DIM
    scale = dk ** -0.5

    @pl.when(c == 0)
    def _():
        ct_ref[...] = jnp.zeros_like(ct_ref)
        n_ref[...] = jnp.zeros_like(n_ref)
        m_ref[...] = jnp.zeros_like(m_ref)

    gc = gcol_ref[0] + bcol_ref[...]
    gr = grow_ref[0] + brow_ref[...]
    lane = lax.broadcasted_iota(jnp.int32, gc.shape, 1)
    sub = lax.broadcasted_iota(jnp.int32, gr.shape, 0)
    tt = lax.broadcasted_iota(jnp.int32, (L, L), 0)
    ss = lax.broadcasted_iota(jnp.int32, (L, L), 1)
    causal = ss <= tt
    tri = jnp.where(causal, 1.0, 0.0).astype(BF16)
    tri_t = jnp.where(tt <= ss, 1.0, 0.0).astype(BF16)

    for hh in range(group):
        h = hp * group + hh
        i_col = jnp.sum(jnp.where(lane == h, gc, 0.0), axis=1, keepdims=True)
        f_col = jnp.sum(jnp.where(lane == heads + h, gc, 0.0), axis=1, keepdims=True)
        i_row = jnp.sum(jnp.where(sub == h, gr, 0.0), axis=0, keepdims=True)
        f_row = jnp.sum(jnp.where(sub == heads + h, gr, 0.0), axis=0, keepdims=True)
        lf_col = _log_sigmoid(f_col)
        lf_row = _log_sigmoid(f_row)

        lf_col_b = jnp.broadcast_to(lf_col, (L, LANES))
        lf_row_b = jnp.broadcast_to(lf_row, (8, L))
        b_col = sum(jnp.dot(tri, part, preferred_element_type=F32) for part in _split3(lf_col_b))[:, 0:1]
        b_row = sum(jnp.dot(part, tri_t, preferred_element_type=F32) for part in _split3(lf_row_b))[0:1, :]
        b_last = b_col[L - 1:L, :]

        m_prev = m_ref[hh]
        logw = jnp.where(causal, b_col - b_row + i_row, NEG)
        log_inter = b_col + m_prev
        m_t = jnp.maximum(log_inter, jnp.max(logw, axis=-1, keepdims=True))
        w_intra = jnp.exp(logw - m_t) * scale
        w_inter = jnp.exp(log_inter - m_t) * scale

        q = q_ref[0, :, hh * dk:(hh + 1) * dk]
        k = k_ref[0, :, hh * dk:(hh + 1) * dk]
        v = v_ref[0, :, hh * dv:(hh + 1) * dv]
        ct = ct_ref[hh]
        n_row = n_ref[hh]
        a = lax.dot_general(q, k, NT_DIMS, preferred_element_type=F32) * w_intra
        num = (jnp.dot(a.astype(BF16), v, preferred_element_type=F32)
               + w_inter * jnp.dot(q, ct.astype(BF16), preferred_element_type=F32))
        den = (jnp.sum(a, axis=-1, keepdims=True)
               + w_inter * jnp.sum(q.astype(F32) * n_row, axis=-1, keepdims=True))
        hx = num / jnp.maximum(jnp.abs(den), jnp.exp(-m_t))

        m_new = m_t[L - 1:L, :]
        wk = jnp.exp(b_last - b_col + i_col - m_new)
        decay = jnp.exp(b_last + m_prev - m_new)
        kw = k.astype(F32) * wk
        ct_ref[hh] = decay * ct + lax.dot_general(kw.astype(BF16), v, (((0,), (0,)), ((), ())),
                                                  preferred_element_type=F32)
        n_ref[hh] = decay * n_row + jnp.sum(kw, axis=0, keepdims=True)
        m_ref[hh] = m_new

        ms = jnp.mean(hx * hx, axis=-1, keepdims=True)
        hn = hx * lax.rsqrt(ms + EPS) * ng_ref[...]
        og = og_ref[0, :, hh * dv:(hh + 1) * dv].astype(F32)
        sg = sgate_ref[0, :, hh * dv:(hh + 1) * dv].astype(F32)
        o_ref[0, :, hh * dv:(hh + 1) * dv] = (hn * jax.nn.sigmoid(og) * _silu(sg)).astype(o_ref.dtype)


def _mlstm(proj, gates, b_gates, norm_g, *, b, s, heads, chunk, group):
    dk, dv = ML_QK_DIM, ML_V_DIM
    L = _tile(s, chunk)
    ng = 2 * heads
    hg = heads // group
    gates_t = jnp.swapaxes(gates, 1, 2)
    kern = functools.partial(_mlstm_kernel, chunk=L, heads=heads, group=group)
    return pl.pallas_call(
        kern,
        out_shape=jax.ShapeDtypeStruct((b, s, heads * dv), BF16),
        grid=(b, hg, s // L),
        in_specs=[
            pl.BlockSpec((1, L, ng), lambda bi, h, c: (bi, c, 0)),
            pl.BlockSpec((1, ng, L), lambda bi, h, c: (bi, 0, c)),
            pl.BlockSpec((1, ng), lambda bi, h, c: (0, 0)),
            pl.BlockSpec((ng, 1), lambda bi, h, c: (0, 0)),
            pl.BlockSpec((1, dv), lambda bi, h, c: (0, 0)),
            pl.BlockSpec((1, L, group * dk), lambda bi, h, c: (bi, c, h)),
            pl.BlockSpec((1, L, group * dk), lambda bi, h, c: (bi, c, hg + h)),
            pl.BlockSpec((1, L, group * dv), lambda bi, h, c: (bi, c, hg + h)),
            pl.BlockSpec((1, L, group * dv), lambda bi, h, c: (bi, c, 2 * hg + h)),
            pl.BlockSpec((1, L, group * dv), lambda bi, h, c: (bi, c, 3 * hg + h)),
        ],
        out_specs=pl.BlockSpec((1, L, group * dv), lambda bi, h, c: (bi, c, h)),
        scratch_shapes=[pltpu.VMEM((group, dk, dv), F32), pltpu.VMEM((group, 1, dk), F32),
                        pltpu.VMEM((group, 1, 1), F32)],
        compiler_params=_cparams(("parallel", "parallel", "arbitrary")),
        name="mlstm",
    )(gates, gates_t, b_gates.reshape(1, ng).astype(F32), b_gates.reshape(ng, 1).astype(F32),
      norm_g.reshape(1, dv).astype(F32), proj, proj, proj, proj, proj)


def _rope_tables(positions, dim):
    inv_freq = ROPE_THETA ** (-jnp.arange(0, dim, 2, dtype=F32) / dim)
    ang = positions.astype(F32).reshape(-1)[:, None] * inv_freq
    cos, sin = jnp.cos(ang), jnp.sin(ang)
    reps = LANES // dim
    return (jnp.tile(jnp.concatenate([cos, cos], axis=-1), (1, reps)),
            jnp.tile(jnp.concatenate([-sin, sin], axis=-1), (1, reps)))


def kernel(x, positions, norm_g, final_g, da_w_in, da_w_out, da_lambda, da_subln_g, sw_w_in, sw_w_out,
           sw_sinks, ml_w_in, ml_b_gates, ml_w_out, ml_norm_g):
    b, s, d = x.shape
    depth = norm_g.shape[0]
    n = b * s
    cos_a, sin_a = _rope_tables(positions, DA_QK_DIM)
    cos_b, sin_b = _rope_tables(positions, SW_HEAD_DIM)
    da_heads = d // DA_V_DIM
    ml_heads = d // ML_V_DIM
    x2d = x.reshape(n, d).astype(F32)
    tm = 1024

    for i in range(depth):
        kind, j = i % N_MIXERS, i // N_MIXERS
        last = i == depth - 1
        if kind == 0:
            lam_init = 0.8 - 0.6 * math.exp(-0.3 * i)
            nq = 2 * da_heads * DA_QK_DIM
            proj = _norm_proj(x2d, norm_g[i], da_w_in[j].astype(BF16), cos_a, sin_a, tm=tm, tn=1024,
                              q_cols=nq, rope_cols=2 * nq, half=DA_QK_DIM // 2, q_scale=LOG2E * DA_QK_DIM ** -0.5)
            o = _diff_attention(proj.reshape(b, s, -1), da_lambda[j], da_subln_g[j], lam_init,
                                b=b, s=s, heads=da_heads, tq=512, tk=512)
            w_out = da_w_out[j]
        elif kind == 1:
            proj = _norm_proj(x2d, norm_g[i], sw_w_in[j].astype(BF16), cos_b, sin_b, tm=tm, tn=512,
                              q_cols=d, rope_cols=d + 4 * SW_HEAD_DIM, half=SW_HEAD_DIM // 2,
                              q_scale=SW_HEAD_DIM ** -0.5)
            o = _sliding_window_attention(proj.reshape(b, s, -1), sw_sinks[j], b=b, s=s)
            w_out = sw_w_out[j]
        else:
            w = ml_w_in[j]
            n_main = 2 * ml_heads * ML_QK_DIM + 2 * d
            ng = 2 * ml_heads
            w_main = jnp.concatenate([w[:, :n_main], w[:, n_main + ng:]], axis=1).astype(BF16)
            w_gate = jnp.pad(w[:, n_main:n_main + ng], ((0, 0), (0, LANES - ng))).astype(BF16)
            proj, gates = _norm_proj(x2d, norm_g[i], w_main, cos_a, sin_a, tm=tm, tn=1024, w_side=w_gate)
            o = _mlstm(proj.reshape(b, s, -1), gates[:, :ng].reshape(b, s, ng), ml_b_gates[j], ml_norm_g[j],
                       b=b, s=s, heads=ml_heads, chunk=256, group=4)
            w_out = ml_w_out[j]
        x2d = _out_proj(o.reshape(n, d), w_out.astype(BF16), x2d, final_g, tm=512, final_norm=last)
    return x2d.reshape(b, s, d).astype(x.dtype)
```

```python
import functools
import math
from typing import NamedTuple

import jax
import jax.numpy as jnp
from jax import lax
from jax.experimental import pallas as pl
from jax.experimental.pallas import tpu as pltpu

F32 = jnp.float32
BF16 = jnp.bfloat16

ROPE_THETA = 10000.0
EPS = 1e-6
NEG = -1e30
N_MIXERS = 3

LANES = 128
MXU_N = 256
VMEM_LIMIT = 56 * 1024 * 1024

DA_QK_DIM = 128
DA_V_DIM = 256
SW_HEAD_DIM = 64
SW_GROUP = 8
SW_WINDOW = 128
ML_QK_DIM = 128
ML_V_DIM = 256
ONES_ROWS = 16
LOG2E = math.log2(math.e)

NT_DIMS = (((1,), (1,)), ((), ()))


def _cparams(sem, flags=None):
    return pltpu.CompilerParams(dimension_semantics=sem, vmem_limit_bytes=VMEM_LIMIT, flags=flags)


def _tile(n, pref):
    t = min(n, pref)
    assert n % t == 0, (n, t)
    return t


class _ProjCfg(NamedTuple):
    tn: int
    n_tiles: int
    q_cols: int
    rope_cols: int
    half: int
    q_scale: float
    has_side: bool


def _rope_chunk(xc, cos, sin, half):
    if 2 * half == LANES:
        partner = pltpu.roll(xc, half, 1)
    else:
        lane = lax.broadcasted_iota(jnp.int32, xc.shape, 1)
        partner = jnp.where(lane % (2 * half) < half, pltpu.roll(xc, LANES - half, 1), pltpu.roll(xc, half, 1))
    return xc * cos + partner * sin


def _norm_proj_kernel(*refs, cfg):
    if cfg.has_side:
        x_ref, g_ref, w_ref, ws_ref, cos_ref, sin_ref, o_ref, side_ref, hn_ref = refs
    else:
        x_ref, g_ref, w_ref, cos_ref, sin_ref, o_ref, hn_ref = refs
    j = pl.program_id(1)

    @pl.when(j == 0)
    def _():
        x = x_ref[...]
        ms = jnp.mean(x * x, axis=-1, keepdims=True)
        hn = (x * lax.rsqrt(ms + EPS) * g_ref[...]).astype(BF16)
        hn_ref[...] = hn
        if cfg.has_side:
            side_ref[...] = jnp.dot(hn, ws_ref[...], preferred_element_type=F32)

    def chunk_mode(col):
        return "q" if col < cfg.q_cols else ("k" if col < cfg.rope_cols else None)

    def emit_tile(modes):
        hn = hn_ref[...]
        if any(modes):
            cos = cos_ref[...]
            sin = sin_ref[...]
        if "q" in modes:
            cos_q = cos * cfg.q_scale
            sin_q = sin * cfg.q_scale
        for cc in range(cfg.tn // MXU_N):
            acc = jnp.dot(hn, w_ref[:, cc * MXU_N:(cc + 1) * MXU_N], preferred_element_type=F32)
            for c in range(MXU_N // LANES):
                mode = modes[cc * (MXU_N // LANES) + c]
                xc = acc[:, c * LANES:(c + 1) * LANES]
                if mode == "q":
                    xc = _rope_chunk(xc, cos_q, sin_q, cfg.half)
                elif mode == "k":
                    xc = _rope_chunk(xc, cos, sin, cfg.half)
                lo = cc * MXU_N + c * LANES
                o_ref[:, lo:lo + LANES] = xc.astype(o_ref.dtype)

    tile_modes = [tuple(chunk_mode(jj * cfg.tn + c * LANES) for c in range(cfg.tn // LANES))
                  for jj in range(cfg.n_tiles)]
    start = 0
    for jj in range(1, cfg.n_tiles + 1):
        if jj == cfg.n_tiles or tile_modes[jj] != tile_modes[start]:
            pl.when((j >= start) & (j < jj))(functools.partial(emit_tile, tile_modes[start]))
            start = jj


def _norm_proj(x2d, g, w, cos, sin, *, tm, tn, q_cols=0, rope_cols=0, half=64, q_scale=1.0, w_side=None):
    n, d = x2d.shape
    dout = w.shape[1]
    tm = _tile(n, tm)
    assert dout % tn == 0 and tn % MXU_N == 0
    cfg = _ProjCfg(tn=tn, n_tiles=dout // tn, q_cols=q_cols, rope_cols=max(rope_cols, q_cols), half=half,
                   q_scale=q_scale, has_side=w_side is not None)
    in_specs = [
        pl.BlockSpec((tm, d), lambda i, j: (i, 0)),
        pl.BlockSpec((1, d), lambda i, j: (0, 0)),
        pl.BlockSpec((d, tn), lambda i, j: (0, j)),
    ]
    args = [x2d, g.reshape(1, d), w]
    out_shape = [jax.ShapeDtypeStruct((n, dout), BF16)]
    out_specs = [pl.BlockSpec((tm, tn), lambda i, j: (i, j))]
    if cfg.has_side:
        in_specs.append(pl.BlockSpec((d, LANES), lambda i, j: (0, 0)))
        args.append(w_side)
        out_shape.append(jax.ShapeDtypeStruct((n, LANES), F32))
        out_specs.append(pl.BlockSpec((tm, LANES), lambda i, j: (i, 0)))
    in_specs += [pl.BlockSpec((tm, LANES), lambda i, j: (i, 0))] * 2
    args += [cos, sin]
    res = pl.pallas_call(
        functools.partial(_norm_proj_kernel, cfg=cfg),
        out_shape=out_shape,
        grid=(n // tm, dout // tn),
        in_specs=in_specs,
        out_specs=out_specs,
        scratch_shapes=[pltpu.VMEM((tm, d), BF16)],
        compiler_params=_cparams(("parallel", "arbitrary")),
        name="norm_proj",
    )(*args)
    return res if cfg.has_side else res[0]


def _out_proj_kernel(o_ref, w_ref, x_ref, fg_ref, out_ref, *, final_norm):
    y = jnp.dot(o_ref[...], w_ref[...], preferred_element_type=F32)
    xn = x_ref[...] + y
    if final_norm:
        ms = jnp.mean(xn * xn, axis=-1, keepdims=True)
        xn = xn * lax.rsqrt(ms + EPS) * fg_ref[...]
    out_ref[...] = xn


def _out_proj(o2d, w, x2d, final_g, *, tm, final_norm):
    n, d = x2d.shape
    k = o2d.shape[1]
    tm = _tile(n, tm)
    return pl.pallas_call(
        functools.partial(_out_proj_kernel, final_norm=final_norm),
        out_shape=jax.ShapeDtypeStruct((n, d), F32),
        grid=(n // tm,),
        in_specs=[
            pl.BlockSpec((tm, k), lambda i: (i, 0)),
            pl.BlockSpec((k, d), lambda i: (0, 0)),
            pl.BlockSpec((tm, d), lambda i: (i, 0)),
            pl.BlockSpec((1, d), lambda i: (0, 0)),
        ],
        out_specs=pl.BlockSpec((tm, d), lambda i: (i, 0)),
        compiler_params=_cparams(("parallel",)),
        name="out_proj",
    )(o2d, w, x2d, final_g.reshape(1, d))


def _silu(g):
    return g * jax.nn.sigmoid(g)


def _da_kernel(lam_ref, sg_ref, q_ref, k_ref, v_ref, g_ref, o_ref, vt_ref, acc_ref, s_ref, p_ref, *,
               tq, tk, lam_init):
    i = pl.program_id(2)
    d = DA_QK_DIM
    dv = DA_V_DIM

    @pl.when(i == 0)
    def _():
        for c in range(vt_ref.shape[0]):
            vt_ref[c, :dv, :] = v_ref[0, c * tk:(c + 1) * tk, :].astype(F32).T.astype(BF16)
            vt_ref[c, dv:, :] = jnp.ones((ONES_ROWS, tk), BF16)

    nqb = q_ref.shape[1] // tq

    def q_block(qi):
        return q_ref[0, pl.ds(pl.multiple_of(qi * tq, tq), tq), :]

    acc_ref[...] = jnp.zeros_like(acc_ref)

    def qk(j, buf, q):
        off = pl.multiple_of(j * tk, tk)
        for mp in range(2):
            kb = k_ref[0, pl.ds(off, tk), mp * d:(mp + 1) * d]
            s_ref[buf, mp] = lax.dot_general(kb, q[:, mp * d:(mp + 1) * d], NT_DIMS,
                                             preferred_element_type=F32)

    def softmax(j, buf, mp, m, masked):
        if masked:
            key = j * tk + lax.broadcasted_iota(jnp.int32, (tk, tq), 0)
            qry = i * tq + lax.broadcasted_iota(jnp.int32, (tk, tq), 1)
            s_ref[buf, mp] = jnp.where(key <= qry, s_ref[buf, mp], NEG)
        m_new = jnp.maximum(m, jnp.max(s_ref[buf, mp], axis=0, keepdims=True))
        p_ref[buf, mp] = jnp.exp2(s_ref[buf, mp] - m_new).astype(BF16)
        return m_new, jnp.exp2(m - m_new)

    def pv(j, buf, mp, alpha):
        acc_ref[mp] = acc_ref[mp] * alpha + jnp.dot(vt_ref[j], p_ref[buf, mp], preferred_element_type=F32)

    def step(j, ms, buf, last):
        if last:
            qk(0, 1 - buf, q_block(jnp.minimum(i + 1, nqb - 1)))
        else:
            qk(j + 1, 1 - buf, q_cur)
        new = []
        for mp in range(2):
            m_new, alpha = softmax(j, buf, mp, ms[mp], last)
            pv(j, buf, mp, alpha)
            new.append(m_new)
        return tuple(new)

    base = ((i * (i + 1)) // 2) % 2

    def any_step(j, ms):
        return lax.cond((j + base) % 2 == 0, lambda ms: step(j, ms, 0, False), lambda ms: step(j, ms, 1, False), ms)

    n_full = i
    q_cur = q_block(i)

    @pl.when(i == 0)
    def _():
        qk(0, 0, q_cur)

    m0 = jnp.full((1, tq), NEG, F32)
    ms = lax.fori_loop(0, n_full, any_step, (m0, m0))
    for parity in range(2):
        @pl.when((n_full + base) % 2 == parity)
        def _(parity=parity):
            step(n_full, ms, parity, True)

    lam = lam_ref[...]
    lam_full = (jnp.exp(jnp.sum(lam[0:1] * lam[1:2], axis=-1, keepdims=True))
                - jnp.exp(jnp.sum(lam[2:3] * lam[3:4], axis=-1, keepdims=True)) + lam_init)
    inv1 = 1.0 / acc_ref[0, dv:dv + 1, :]
    inv2 = 1.0 / acc_ref[1, dv:dv + 1, :]
    o_t = acc_ref[0, :dv, :] * inv1 - lam_full * (acc_ref[1, :dv, :] * inv2)
    o = o_t.T
    ms = jnp.mean(o * o, axis=-1, keepdims=True)
    o = o * lax.rsqrt(ms + EPS) * sg_ref[...] * (1.0 - lam_init)
    o = o * _silu(g_ref[0].astype(F32))
    o_ref[0] = o.astype(o_ref.dtype)


def _diff_attention(proj, lam, subln_g, lam_init, *, b, s, heads, tq, tk):
    dv = DA_V_DIM
    tq, tk = _tile(s, tq), _tile(s, tk)
    assert tk == tq
    kern = functools.partial(_da_kernel, tq=tq, tk=tk, lam_init=lam_init)
    return pl.pallas_call(
        kern,
        out_shape=jax.ShapeDtypeStruct((b, s, heads * dv), BF16),
        grid=(b, heads, s // tq),
        in_specs=[
            pl.BlockSpec((4, DA_QK_DIM), lambda bi, h, i: (0, 0)),
            pl.BlockSpec((1, dv), lambda bi, h, i: (0, 0)),
            pl.BlockSpec((1, s, dv), lambda bi, h, i: (bi, 0, h)),
            pl.BlockSpec((1, s, dv), lambda bi, h, i: (bi, 0, heads + h)),
            pl.BlockSpec((1, s, dv), lambda bi, h, i: (bi, 0, 2 * heads + h)),
            pl.BlockSpec((1, tq, dv), lambda bi, h, i: (bi, i, 3 * heads + h)),
        ],
        out_specs=pl.BlockSpec((1, tq, dv), lambda bi, h, i: (bi, i, h)),
        scratch_shapes=[pltpu.VMEM((s // tk, dv + ONES_ROWS, tk), BF16), pltpu.VMEM((2, dv + ONES_ROWS, tq), F32),
                        pltpu.VMEM((2, 2, tk, tq), F32), pltpu.VMEM((2, 2, tk, tq), BF16)],
        compiler_params=_cparams(("parallel", "parallel", "arbitrary")),
        name="diff_attention",
    )(lam.astype(F32), subln_g.reshape(1, dv).astype(F32), proj, proj, proj, proj)


def _swa_kernel(sink_ref, q_ref, kvc_ref, kvp_ref, g0_ref, g1_ref, g2_ref, g3_ref, o_ref):
    i = pl.program_id(1)
    w = SW_WINDOW
    hd = SW_HEAD_DIM
    nkv = 4
    g_refs = (g0_ref, g1_ref, g2_ref, g3_ref)
    has_prev = i > 0

    lane = lax.broadcasted_iota(jnp.int32, (w, LANES), 1)
    lo = lane < hd
    rows = 4 * w
    row = lax.broadcasted_iota(jnp.int32, (rows, w), 0) % w
    col = lax.broadcasted_iota(jnp.int32, (rows, w), 1)
    lower = col <= row
    ones_w = jnp.ones((w, LANES), BF16)

    def split_pair(x_bf16, parity):
        xf = x_bf16.astype(F32)
        if parity == 0:
            a = jnp.where(lo, xf, 0.0)
            bb = pltpu.roll(a, hd, 1)
        else:
            bb = jnp.where(lo, 0.0, xf)
            a = pltpu.roll(bb, hd, 1)
        return a.astype(BF16), bb.astype(BF16)

    kvc = kvc_ref[0]
    kvp = kvp_ref[0]
    q_all = q_ref[0]
    for kvh in range(nkv):
        c, parity = kvh // 2, kvh % 2
        kc = split_pair(kvc[:, c * LANES:(c + 1) * LANES], parity)
        kp = split_pair(kvp[:, c * LANES:(c + 1) * LANES], parity)
        vc = split_pair(kvc[:, (2 + c) * LANES:(3 + c) * LANES], parity)
        vp = split_pair(kvp[:, (2 + c) * LANES:(3 + c) * LANES], parity)
        base = kvh * SW_GROUP * hd
        q4 = jnp.concatenate([q_all[:, base + jj * LANES: base + (jj + 1) * LANES] for jj in range(4)], axis=0)
        o4 = jnp.zeros((rows, LANES), F32)
        for ab in range(2):
            k2 = jnp.concatenate([kp[ab], kc[ab]], axis=0)
            v2 = jnp.concatenate([vp[ab], vc[ab]], axis=0)
            s2 = lax.dot_general(q4, k2, NT_DIMS, preferred_element_type=F32)
            s = jnp.where(lower, s2[:, w:], jnp.where(has_prev, s2[:, :w], NEG))
            sink = jnp.concatenate(
                [jnp.full((w, LANES), sink_ref[kvh * SW_GROUP + 2 * jj + ab], F32) for jj in range(4)], axis=0)
            m = jnp.maximum(jnp.max(s, axis=-1, keepdims=True), sink)
            e = jnp.exp(s - m)
            p2 = jnp.concatenate([jnp.where(lower, 0.0, e), jnp.where(lower, e, 0.0)], axis=1).astype(BF16)
            denom = jnp.dot(e.astype(BF16), ones_w, preferred_element_type=F32) + jnp.exp(sink - m)
            o4 = o4 + jnp.dot(p2, v2, preferred_element_type=F32) * (1.0 / denom)
        gate = g_refs[kvh][0].astype(F32)
        for jj in range(4):
            og = o4[jj * w:(jj + 1) * w] * _silu(gate[:, jj * LANES:(jj + 1) * LANES])
            o_ref[0, :, base + jj * LANES: base + (jj + 1) * LANES] = og.astype(o_ref.dtype)


def _sliding_window_attention(proj, sinks, *, b, s):
    w = SW_WINDOW
    width = 32 * SW_HEAD_DIM
    gw = SW_GROUP * SW_HEAD_DIM
    kv_blk = width // gw
    g_specs = [pl.BlockSpec((1, w, gw), functools.partial(lambda bi, i, kk: (bi, i, kk), kk=kv_blk + 1 + kvh))
               for kvh in range(4)]
    return pl.pallas_call(
        _swa_kernel,
        out_shape=jax.ShapeDtypeStruct((b, s, width), BF16),
        grid=(b, s // w),
        in_specs=[
            pl.BlockSpec(memory_space=pltpu.SMEM),
            pl.BlockSpec((1, w, width), lambda bi, i: (bi, i, 0)),
            pl.BlockSpec((1, w, gw), lambda bi, i: (bi, i, kv_blk)),
            pl.BlockSpec((1, w, gw), lambda bi, i: (bi, jnp.maximum(i - 1, 0), kv_blk)),
        ] + g_specs,
        out_specs=pl.BlockSpec((1, w, width), lambda bi, i: (bi, i, 0)),
        compiler_params=_cparams(("parallel", "arbitrary")),
        name="sliding_window_attention",
    )(sinks.astype(F32), proj, proj, proj, proj, proj, proj, proj)


def _log_sigmoid(x):
    return jnp.minimum(x, 0.0) - jnp.log1p(jnp.exp(-jnp.abs(x)))


def _split3(x):
    hi = x.astype(BF16)
    r1 = x - hi.astype(F32)
    mid = r1.astype(BF16)
    lo = (r1 - mid.astype(F32)).astype(BF16)
    return hi, mid, lo


def _mlstm_kernel(gcol_ref, grow_ref, bcol_ref, brow_ref, ng_ref, q_ref, k_ref, v_ref, og_ref, sgate_ref,
                  o_ref, ct_ref, n_ref, m_ref, *, chunk, heads, group):
    hp = pl.program_id(1)
    c = pl.program_id(2)
    L = chunk
    dk, dv = ML_QK_DIM, ML_V_DIM
    scale = dk ** -0.5

    @pl.when(c == 0)
    def _():
        ct_ref[...] = jnp.zeros_like(ct_ref)
        n_ref[...] = jnp.zeros_like(n_ref)
        m_ref[...] = jnp.zeros_like(m_ref)

    gc = gcol_ref[0] + bcol_ref[...]
    gr = grow_ref[0] + brow_ref[...]
    lane = lax.broadcasted_iota(jnp.int32, gc.shape, 1)
    sub = lax.broadcasted_iota(jnp.int32, gr.shape, 0)
    tt = lax.broadcasted_iota(jnp.int32, (L, L), 0)
    ss = lax.broadcasted_iota(jnp.int32, (L, L), 1)
    causal = ss <= tt
    tri = jnp.where(causal, 1.0, 0.0).astype(BF16)
    tri_t = jnp.where(tt <= ss, 1.0, 0.0).astype(BF16)

    for hh in range(group):
        h = hp * group + hh
        i_col = jnp.sum(jnp.where(lane == h, gc, 0.0), axis=1, keepdims=True)
        f_col = jnp.sum(jnp.where(lane == heads + h, gc, 0.0), axis=1, keepdims=True)
        i_row = jnp.sum(jnp.where(sub == h, gr, 0.0), axis=0, keepdims=True)
        f_row = jnp.sum(jnp.where(sub == heads + h, gr, 0.0), axis=0, keepdims=True)
        lf_col = _log_sigmoid(f_col)
        lf_row = _log_sigmoid(f_row)

        lf_col_b = jnp.broadcast_to(lf_col, (L, LANES))
        lf_row_b = jnp.broadcast_to(lf_row, (8, L))
        b_col = sum(jnp.dot(tri, part, preferred_element_type=F32) for part in _split3(lf_col_b))[:, 0:1]
        b_row = sum(jnp.dot(part, tri_t, preferred_element_type=F32) for part in _split3(lf_row_b))[0:1, :]
        b_last = b_col[L - 1:L, :]

        m_prev = m_ref[hh]
        logw = jnp.where(causal, b_col - b_row + i_row, NEG)
        log_inter = b_col + m_prev
        m_t = jnp.maximum(log_inter, jnp.max(logw, axis=-1, keepdims=True))
        w_intra = jnp.exp(logw - m_t) * scale
        w_inter = jnp.exp(log_inter - m_t) * scale

        q = q_ref[0, :, hh * dk:(hh + 1) * dk]
        k = k_ref[0, :, hh * dk:(hh + 1) * dk]
        v = v_ref[0, :, hh * dv:(hh + 1) * dv]
        ct = ct_ref[hh]
        n_row = n_ref[hh]
        a = lax.dot_general(q, k, NT_DIMS, preferred_element_type=F32) * w_intra
        num = (jnp.dot(a.astype(BF16), v, preferred_element_type=F32)
               + w_inter * jnp.dot(q, ct.astype(BF16), preferred_element_type=F32))
        den = (jnp.sum(a, axis=-1, keepdims=True)
               + w_inter * jnp.sum(q.astype(F32) * n_row, axis=-1, keepdims=True))
        hx = num / jnp.maximum(jnp.abs(den), jnp.exp(-m_t))

        m_new = m_t[L - 1:L, :]
        wk = jnp.exp(b_last - b_col + i_col - m_new)
        decay = jnp.exp(b_last + m_prev - m_new)
        kw = k.astype(F32) * wk
        ct_ref[hh] = decay * ct + lax.dot_general(kw.astype(BF16), v, (((0,), (0,)), ((), ())),
                                                  preferred_element_type=F32)
        n_ref[hh] = decay * n_row + jnp.sum(kw, axis=0, keepdims=True)
        m_ref[hh] = m_new

        ms = jnp.mean(hx * hx, axis=-1, keepdims=True)
        hn = hx * lax.rsqrt(ms + EPS) * ng_ref[...]
        og = og_ref[0, :, hh * dv:(hh + 1) * dv].astype(F32)
        sg = sgate_ref[0, :, hh * dv:(hh + 1) * dv].astype(F32)
        o_ref[0, :, hh * dv:(hh + 1) * dv] = (hn * jax.nn.sigmoid(og) * _silu(sg)).astype(o_ref.dtype)


def _mlstm(proj, gates, b_gates, norm_g, *, b, s, heads, chunk, group):
    dk, dv = ML_QK_DIM, ML_V_DIM
    L = _tile(s, chunk)
    ng = 2 * heads
    hg = heads // group
    gates_t = jnp.swapaxes(gates, 1, 2)
    kern = functools.partial(_mlstm_kernel, chunk=L, heads=heads, group=group)
    return pl.pallas_call(
        kern,
        out_shape=jax.ShapeDtypeStruct((b, s, heads * dv), BF16),
        grid=(b, hg, s // L),
        in_specs=[
            pl.BlockSpec((1, L, ng), lambda bi, h, c: (bi, c, 0)),
            pl.BlockSpec((1, ng, L), lambda bi, h, c: (bi, 0, c)),
            pl.BlockSpec((1, ng), lambda bi, h, c: (0, 0)),
            pl.BlockSpec((ng, 1), lambda bi, h, c: (0, 0)),
            pl.BlockSpec((1, dv), lambda bi, h, c: (0, 0)),
            pl.BlockSpec((1, L, group * dk), lambda bi, h, c: (bi, c, h)),
            pl.BlockSpec((1, L, group * dk), lambda bi, h, c: (bi, c, hg + h)),
            pl.BlockSpec((1, L, group * dv), lambda bi, h, c: (bi, c, hg + h)),
            pl.BlockSpec((1, L, group * dv), lambda bi, h, c: (bi, c, 2 * hg + h)),
            pl.BlockSpec((1, L, group * dv), lambda bi, h, c: (bi, c, 3 * hg + h)),
        ],
        out_specs=pl.BlockSpec((1, L, group * dv), lambda bi, h, c: (bi, c, h)),
        scratch_shapes=[pltpu.VMEM((group, dk, dv), F32), pltpu.VMEM((group, 1, dk), F32),
                        pltpu.VMEM((group, 1, 1), F32)],
        compiler_params=_cparams(("parallel", "parallel", "arbitrary")),
        name="mlstm",
    )(gates, gates_t, b_gates.reshape(1, ng).astype(F32), b_gates.reshape(ng, 1).astype(F32),
      norm_g.reshape(1, dv).astype(F32), proj, proj, proj, proj, proj)


def _rope_tables(positions, dim):
    inv_freq = ROPE_THETA ** (-jnp.arange(0, dim, 2, dtype=F32) / dim)
    ang = positions.astype(F32).reshape(-1)[:, None] * inv_freq
    cos, sin = jnp.cos(ang), jnp.sin(ang)
    reps = LANES // dim
    return (jnp.tile(jnp.concatenate([cos, cos], axis=-1), (1, reps)),
            jnp.tile(jnp.concatenate([-sin, sin], axis=-1), (1, reps)))


def kernel(x, positions, norm_g, final_g, da_w_in, da_w_out, da_lambda, da_subln_g, sw_w_in, sw_w_out,
           sw_sinks, ml_w_in, ml_b_gates, ml_w_out, ml_norm_g):
    b, s, d = x.shape
    depth = norm_g.shape[0]
    n = b * s
    cos_a, sin_a = _rope_tables(positions, DA_QK_DIM)
    cos_b, sin_b = _rope_tables(positions, SW_HEAD_DIM)
    da_heads = d // DA_V_DIM
    ml_heads = d // ML_V_DIM
    x2d = x.reshape(n, d).astype(F32)
    tm = 1024

    for i in range(depth):
        kind, j = i % N_MIXERS, i // N_MIXERS
        last = i == depth - 1
        if kind == 0:
            lam_init = 0.8 - 0.6 * math.exp(-0.3 * i)
            nq = 2 * da_heads * DA_QK_DIM
            proj = _norm_proj(x2d, norm_g[i], da_w_in[j].astype(BF16), cos_a, sin_a, tm=tm, tn=1024,
                              q_cols=nq, rope_cols=2 * nq, half=DA_QK_DIM // 2, q_scale=LOG2E * DA_QK_DIM ** -0.5)
            o = _diff_attention(proj.reshape(b, s, -1), da_lambda[j], da_subln_g[j], lam_init,
                                b=b, s=s, heads=da_heads, tq=512, tk=512)
            w_out = da_w_out[j]
        elif kind == 1:
            proj = _norm_proj(x2d, norm_g[i], sw_w_in[j].astype(BF16), cos_b, sin_b, tm=tm, tn=1536,
                              q_cols=d, rope_cols=d + 4 * SW_HEAD_DIM, half=SW_HEAD_DIM // 2,
                              q_scale=SW_HEAD_DIM ** -0.5)
            o = _sliding_window_attention(proj.reshape(b, s, -1), sw_sinks[j], b=b, s=s)
            w_out = sw_w_out[j]
        else:
            w = ml_w_in[j]
            n_main = 2 * ml_heads * ML_QK_DIM + 2 * d
            ng = 2 * ml_heads
            w_main = jnp.concatenate([w[:, :n_main], w[:, n_main + ng:]], axis=1).astype(BF16)
            w_gate = jnp.pad(w[:, n_main:n_main + ng], ((0, 0), (0, LANES - ng))).astype(BF16)
            proj, gates = _norm_proj(x2d, norm_g[i], w_main, cos_a, sin_a, tm=tm, tn=1024, w_side=w_gate)
            o = _mlstm(proj.reshape(b, s, -1), gates[:, :ng].reshape(b, s, ng), ml_b_gates[j], ml_norm_g[j],
                       b=b, s=s, heads=ml_heads, chunk=256, group=4)
            w_out = ml_w_out[j]
        x2d = _out_proj(o.reshape(n, d), w_out.astype(BF16), x2d, final_g, tm=512, final_norm=last)
    return x2d.reshape(b, s, d).astype(x.dtype)
```

```python
import functools
import math
from typing import NamedTuple

import jax
import jax.numpy as jnp
from jax import lax
from jax.experimental import pallas as pl
from jax.experimental.pallas import tpu as pltpu

F32 = jnp.float32
BF16 = jnp.bfloat16

ROPE_THETA = 10000.0
EPS = 1e-6
NEG = -1e30
N_MIXERS = 3

LANES = 128
MXU_N = 256
VMEM_LIMIT = 56 * 1024 * 1024

DA_QK_DIM = 128
DA_V_DIM = 256
SW_HEAD_DIM = 64
SW_GROUP = 8
SW_WINDOW = 128
ML_QK_DIM = 128
ML_V_DIM = 256
ONES_ROWS = 16
LOG2E = math.log2(math.e)

NT_DIMS = (((1,), (1,)), ((), ()))


def _cparams(sem, flags=None):
    return pltpu.CompilerParams(dimension_semantics=sem, vmem_limit_bytes=VMEM_LIMIT, flags=flags)


def _tile(n, pref):
    t = min(n, pref)
    assert n % t == 0, (n, t)
    return t


class _ProjCfg(NamedTuple):
    tn: int
    n_tiles: int
    q_cols: int
    rope_cols: int
    half: int
    q_scale: float
    has_side: bool


def _rope_chunk(xc, cos, sin, half):
    if 2 * half == LANES:
        partner = pltpu.roll(xc, half, 1)
    else:
        lane = lax.broadcasted_iota(jnp.int32, xc.shape, 1)
        partner = jnp.where(lane % (2 * half) < half, pltpu.roll(xc, LANES - half, 1), pltpu.roll(xc, half, 1))
    return xc * cos + partner * sin


def _norm_proj_kernel(*refs, cfg):
    if cfg.has_side:
        x_ref, g_ref, w_ref, ws_ref, cos_ref, sin_ref, o_ref, side_ref, hn_ref = refs
    else:
        x_ref, g_ref, w_ref, cos_ref, sin_ref, o_ref, hn_ref = refs
    j = pl.program_id(1)

    @pl.when(j == 0)
    def _():
        x = x_ref[...]
        ms = jnp.mean(x * x, axis=-1, keepdims=True)
        hn = (x * lax.rsqrt(ms + EPS) * g_ref[...]).astype(BF16)
        hn_ref[...] = hn
        if cfg.has_side:
            side_ref[...] = jnp.dot(hn, ws_ref[...], preferred_element_type=F32)

    def chunk_mode(col):
        return "q" if col < cfg.q_cols else ("k" if col < cfg.rope_cols else None)

    def emit_tile(modes):
        hn = hn_ref[...]
        if any(modes):
            cos = cos_ref[...]
            sin = sin_ref[...]
        if "q" in modes:
            cos_q = cos * cfg.q_scale
            sin_q = sin * cfg.q_scale
        for cc in range(cfg.tn // MXU_N):
            acc = jnp.dot(hn, w_ref[:, cc * MXU_N:(cc + 1) * MXU_N], preferred_element_type=F32)
            for c in range(MXU_N // LANES):
                mode = modes[cc * (MXU_N // LANES) + c]
                xc = acc[:, c * LANES:(c + 1) * LANES]
                if mode == "q":
                    xc = _rope_chunk(xc, cos_q, sin_q, cfg.half)
                elif mode == "k":
                    xc = _rope_chunk(xc, cos, sin, cfg.half)
                lo = cc * MXU_N + c * LANES
                o_ref[:, lo:lo + LANES] = xc.astype(o_ref.dtype)

    tile_modes = [tuple(chunk_mode(jj * cfg.tn + c * LANES) for c in range(cfg.tn // LANES))
                  for jj in range(cfg.n_tiles)]
    start = 0
    for jj in range(1, cfg.n_tiles + 1):
        if jj == cfg.n_tiles or tile_modes[jj] != tile_modes[start]:
            pl.when((j >= start) & (j < jj))(functools.partial(emit_tile, tile_modes[start]))
            start = jj


def _norm_proj(x2d, g, w, cos, sin, *, tm, tn, q_cols=0, rope_cols=0, half=64, q_scale=1.0, w_side=None):
    n, d = x2d.shape
    dout = w.shape[1]
    tm = _tile(n, tm)
    assert dout % tn == 0 and tn % MXU_N == 0
    cfg = _ProjCfg(tn=tn, n_tiles=dout // tn, q_cols=q_cols, rope_cols=max(rope_cols, q_cols), half=half,
                   q_scale=q_scale, has_side=w_side is not None)
    in_specs = [
        pl.BlockSpec((tm, d), lambda i, j: (i, 0)),
        pl.BlockSpec((1, d), lambda i, j: (0, 0)),
        pl.BlockSpec((d, tn), lambda i, j: (0, j)),
    ]
    args = [x2d, g.reshape(1, d), w]
    out_shape = [jax.ShapeDtypeStruct((n, dout), BF16)]
    out_specs = [pl.BlockSpec((tm, tn), lambda i, j: (i, j))]
    if cfg.has_side:
        in_specs.append(pl.BlockSpec((d, LANES), lambda i, j: (0, 0)))
        args.append(w_side)
        out_shape.append(jax.ShapeDtypeStruct((n, LANES), F32))
        out_specs.append(pl.BlockSpec((tm, LANES), lambda i, j: (i, 0)))
    in_specs += [pl.BlockSpec((tm, LANES), lambda i, j: (i, 0))] * 2
    args += [cos, sin]
    res = pl.pallas_call(
        functools.partial(_norm_proj_kernel, cfg=cfg),
        out_shape=out_shape,
        grid=(n // tm, dout // tn),
        in_specs=in_specs,
        out_specs=out_specs,
        scratch_shapes=[pltpu.VMEM((tm, d), BF16)],
        compiler_params=_cparams(("parallel", "arbitrary")),
        name="norm_proj",
    )(*args)
    return res if cfg.has_side else res[0]


def _out_proj_kernel(o_ref, w_ref, x_ref, fg_ref, out_ref, *, final_norm):
    y = jnp.dot(o_ref[...], w_ref[...], preferred_element_type=F32)
    xn = x_ref[...] + y
    if final_norm:
        ms = jnp.mean(xn * xn, axis=-1, keepdims=True)
        xn = xn * lax.rsqrt(ms + EPS) * fg_ref[...]
    out_ref[...] = xn


def _out_proj(o2d, w, x2d, final_g, *, tm, final_norm):
    n, d = x2d.shape
    k = o2d.shape[1]
    tm = _tile(n, tm)
    return pl.pallas_call(
        functools.partial(_out_proj_kernel, final_norm=final_norm),
        out_shape=jax.ShapeDtypeStruct((n, d), F32),
        grid=(n // tm,),
        in_specs=[
            pl.BlockSpec((tm, k), lambda i: (i, 0)),
            pl.BlockSpec((k, d), lambda i: (0, 0)),
            pl.BlockSpec((tm, d), lambda i: (i, 0)),
            pl.BlockSpec((1, d), lambda i: (0, 0)),
        ],
        out_specs=pl.BlockSpec((tm, d), lambda i: (i, 0)),
        compiler_params=_cparams(("parallel",)),
        name="out_proj",
    )(o2d, w, x2d, final_g.reshape(1, d))


def _silu(g):
    return g * jax.nn.sigmoid(g)


def _da_kernel(lam_ref, sg_ref, q_ref, k_ref, v_ref, g_ref, o_ref, vt_ref, acc_ref, s_ref, p_ref, *,
               tq, tk, lam_init):
    i = pl.program_id(2)
    d = DA_QK_DIM
    dv = DA_V_DIM

    @pl.when(i == 0)
    def _():
        for c in range(vt_ref.shape[0]):
            vt_ref[c, :dv, :] = v_ref[0, c * tk:(c + 1) * tk, :].astype(F32).T.astype(BF16)
            vt_ref[c, dv:, :] = jnp.ones((ONES_ROWS, tk), BF16)

    nqb = q_ref.shape[1] // tq

    def q_block(qi):
        return q_ref[0, pl.ds(pl.multiple_of(qi * tq, tq), tq), :]

    acc_ref[...] = jnp.zeros_like(acc_ref)

    def qk(j, buf, q):
        off = pl.multiple_of(j * tk, tk)
        for mp in range(2):
            kb = k_ref[0, pl.ds(off, tk), mp * d:(mp + 1) * d]
            s_ref[buf, mp] = lax.dot_general(kb, q[:, mp * d:(mp + 1) * d], NT_DIMS,
                                             preferred_element_type=F32)

    def softmax(j, buf, mp, m, masked):
        if masked:
            key = j * tk + lax.broadcasted_iota(jnp.int32, (tk, tq), 0)
            qry = i * tq + lax.broadcasted_iota(jnp.int32, (tk, tq), 1)
            s_ref[buf, mp] = jnp.where(key <= qry, s_ref[buf, mp], NEG)
        m_new = jnp.maximum(m, jnp.max(s_ref[buf, mp], axis=0, keepdims=True))
        p_ref[buf, mp] = jnp.exp2(s_ref[buf, mp] - m_new).astype(BF16)
        return m_new, jnp.exp2(m - m_new)

    def pv(j, buf, mp, alpha):
        acc_ref[mp] = acc_ref[mp] * alpha + jnp.dot(vt_ref[j], p_ref[buf, mp], preferred_element_type=F32)

    def step(j, ms, buf, last):
        if last:
            qk(0, 1 - buf, q_block(jnp.minimum(i + 1, nqb - 1)))
        else:
            qk(j + 1, 1 - buf, q_cur)
        new = []
        for mp in range(2):
            m_new, alpha = softmax(j, buf, mp, ms[mp], last)
            pv(j, buf, mp, alpha)
            new.append(m_new)
        return tuple(new)

    base = ((i * (i + 1)) // 2) % 2

    def any_step(j, ms):
        return lax.cond((j + base) % 2 == 0, lambda ms: step(j, ms, 0, False), lambda ms: step(j, ms, 1, False), ms)

    n_full = i
    q_cur = q_block(i)

    @pl.when(i == 0)
    def _():
        qk(0, 0, q_cur)

    m0 = jnp.full((1, tq), NEG, F32)
    ms = lax.fori_loop(0, n_full, any_step, (m0, m0))
    for parity in range(2):
        @pl.when((n_full + base) % 2 == parity)
        def _(parity=parity):
            step(n_full, ms, parity, True)

    lam = lam_ref[...]
    lam_full = (jnp.exp(jnp.sum(lam[0:1] * lam[1:2], axis=-1, keepdims=True))
                - jnp.exp(jnp.sum(lam[2:3] * lam[3:4], axis=-1, keepdims=True)) + lam_init)
    inv1 = 1.0 / acc_ref[0, dv:dv + 1, :]
    inv2 = 1.0 / acc_ref[1, dv:dv + 1, :]
    o_t = acc_ref[0, :dv, :] * inv1 - lam_full * (acc_ref[1, :dv, :] * inv2)
    o = o_t.T
    ms = jnp.mean(o * o, axis=-1, keepdims=True)
    o = o * lax.rsqrt(ms + EPS) * sg_ref[...] * (1.0 - lam_init)
    o = o * _silu(g_ref[0].astype(F32))
    o_ref[0] = o.astype(o_ref.dtype)


def _diff_attention(proj, lam, subln_g, lam_init, *, b, s, heads, tq, tk):
    dv = DA_V_DIM
    tq, tk = _tile(s, tq), _tile(s, tk)
    assert tk == tq
    kern = functools.partial(_da_kernel, tq=tq, tk=tk, lam_init=lam_init)
    return pl.pallas_call(
        kern,
        out_shape=jax.ShapeDtypeStruct((b, s, heads * dv), BF16),
        grid=(b, heads, s // tq),
        in_specs=[
            pl.BlockSpec((4, DA_QK_DIM), lambda bi, h, i: (0, 0)),
            pl.BlockSpec((1, dv), lambda bi, h, i: (0, 0)),
            pl.BlockSpec((1, s, dv), lambda bi, h, i: (bi, 0, h)),
            pl.BlockSpec((1, s, dv), lambda bi, h, i: (bi, 0, heads + h)),
            pl.BlockSpec((1, s, dv), lambda bi, h, i: (bi, 0, 2 * heads + h)),
            pl.BlockSpec((1, tq, dv), lambda bi, h, i: (bi, i, 3 * heads + h)),
        ],
        out_specs=pl.BlockSpec((1, tq, dv), lambda bi, h, i: (bi, i, h)),
        scratch_shapes=[pltpu.VMEM((s // tk, dv + ONES_ROWS, tk), BF16), pltpu.VMEM((2, dv + ONES_ROWS, tq), F32),
                        pltpu.VMEM((2, 2, tk, tq), F32), pltpu.VMEM((2, 2, tk, tq), BF16)],
        compiler_params=_cparams(("parallel", "parallel", "arbitrary")),
        name="diff_attention",
    )(lam.astype(F32), subln_g.reshape(1, dv).astype(F32), proj, proj, proj, proj)


def _swa_kernel(sink_ref, q_ref, kvc_ref, kvp_ref, g0_ref, g1_ref, g2_ref, g3_ref, o_ref):
    i = pl.program_id(1)
    w = SW_WINDOW
    hd = SW_HEAD_DIM
    nkv = 4
    g_refs = (g0_ref, g1_ref, g2_ref, g3_ref)
    has_prev = i > 0

    lane = lax.broadcasted_iota(jnp.int32, (w, LANES), 1)
    lo = lane < hd
    rows = 4 * w
    row = lax.broadcasted_iota(jnp.int32, (rows, w), 0) % w
    col = lax.broadcasted_iota(jnp.int32, (rows, w), 1)
    lower = col <= row
    ones_w = jnp.ones((w, LANES), BF16)

    def split_pair(x_bf16, parity):
        xf = x_bf16.astype(F32)
        if parity == 0:
            a = jnp.where(lo, xf, 0.0)
            bb = pltpu.roll(a, hd, 1)
        else:
            bb = jnp.where(lo, 0.0, xf)
            a = pltpu.roll(bb, hd, 1)
        return a.astype(BF16), bb.astype(BF16)

    kvc = kvc_ref[0]
    kvp = kvp_ref[0]
    q_all = q_ref[0]
    for kvh in range(nkv):
        c, parity = kvh // 2, kvh % 2
        kc = split_pair(kvc[:, c * LANES:(c + 1) * LANES], parity)
        kp = split_pair(kvp[:, c * LANES:(c + 1) * LANES], parity)
        vc = split_pair(kvc[:, (2 + c) * LANES:(3 + c) * LANES], parity)
        vp = split_pair(kvp[:, (2 + c) * LANES:(3 + c) * LANES], parity)
        base = kvh * SW_GROUP * hd
        q4 = jnp.concatenate([q_all[:, base + jj * LANES: base + (jj + 1) * LANES] for jj in range(4)], axis=0)
        o4 = jnp.zeros((rows, LANES), F32)
        for ab in range(2):
            k2 = jnp.concatenate([kp[ab], kc[ab]], axis=0)
            v2 = jnp.concatenate([vp[ab], vc[ab]], axis=0)
            s2 = lax.dot_general(q4, k2, NT_DIMS, preferred_element_type=F32)
            s = jnp.where(lower, s2[:, w:], jnp.where(has_prev, s2[:, :w], NEG))
            sink = jnp.concatenate(
                [jnp.full((w, LANES), sink_ref[kvh * SW_GROUP + 2 * jj + ab], F32) for jj in range(4)], axis=0)
            m = jnp.maximum(jnp.max(s, axis=-1, keepdims=True), sink)
            e = jnp.exp(s - m)
            p2 = jnp.concatenate([jnp.where(lower, 0.0, e), jnp.where(lower, e, 0.0)], axis=1).astype(BF16)
            denom = jnp.dot(e.astype(BF16), ones_w, preferred_element_type=F32) + jnp.exp(sink - m)
            o4 = o4 + jnp.dot(p2, v2, preferred_element_type=F32) * (1.0 / denom)
        gate = g_refs[kvh][0].astype(F32)
        for jj in range(4):
            og = o4[jj * w:(jj + 1) * w] * _silu(gate[:, jj * LANES:(jj + 1) * LANES])
            o_ref[0, :, base + jj * LANES: base + (jj + 1) * LANES] = og.astype(o_ref.dtype)


def _sliding_window_attention(proj, sinks, *, b, s):
    w = SW_WINDOW
    width = 32 * SW_HEAD_DIM
    gw = SW_GROUP * SW_HEAD_DIM
    kv_blk = width // gw
    g_specs = [pl.BlockSpec((1, w, gw), functools.partial(lambda bi, i, kk: (bi, i, kk), kk=kv_blk + 1 + kvh))
               for kvh in range(4)]
    return pl.pallas_call(
        _swa_kernel,
        out_shape=jax.ShapeDtypeStruct((b, s, width), BF16),
        grid=(b, s // w),
        in_specs=[
            pl.BlockSpec(memory_space=pltpu.SMEM),
            pl.BlockSpec((1, w, width), lambda bi, i: (bi, i, 0)),
            pl.BlockSpec((1, w, gw), lambda bi, i: (bi, i, kv_blk)),
            pl.BlockSpec((1, w, gw), lambda bi, i: (bi, jnp.maximum(i - 1, 0), kv_blk)),
        ] + g_specs,
        out_specs=pl.BlockSpec((1, w, width), lambda bi, i: (bi, i, 0)),
        compiler_params=_cparams(("parallel", "arbitrary")),
        name="sliding_window_attention",
    )(sinks.astype(F32), proj, proj, proj, proj, proj, proj, proj)


def _log_sigmoid(x):
    return jnp.minimum(x, 0.0) - jnp.log1p(jnp.exp(-jnp.abs(x)))


def _split3(x):
    hi = x.astype(BF16)
    r1 = x - hi.astype(F32)
    mid = r1.astype(BF16)
    lo = (r1 - mid.astype(F32)).astype(BF16)
    return hi, mid, lo


def _mlstm_kernel(gcol_ref, grow_ref, bcol_ref, brow_ref, ng_ref, q_ref, k_ref, v_ref, og_ref, sgate_ref,
                  o_ref, ct_ref, n_ref, m_ref, *, chunk, heads, group):
    hp = pl.program_id(1)
    c = pl.program_id(2)
    L = chunk
    dk, dv = ML_QK_DIM, ML_V_DIM
    scale = dk ** -0.5

    @pl.when(c == 0)
    def _():
        ct_ref[...] = jnp.zeros_like(ct_ref)
        n_ref[...] = jnp.zeros_like(n_ref)
        m_ref[...] = jnp.zeros_like(m_ref)

    gc = gcol_ref[0] + bcol_ref[...]
    gr = grow_ref[0] + brow_ref[...]
    lane = lax.broadcasted_iota(jnp.int32, gc.shape, 1)
    sub = lax.broadcasted_iota(jnp.int32, gr.shape, 0)
    tt = lax.broadcasted_iota(jnp.int32, (L, L), 0)
    ss = lax.broadcasted_iota(jnp.int32, (L, L), 1)
    causal = ss <= tt
    tri = jnp.where(causal, 1.0, 0.0).astype(BF16)
    tri_t = jnp.where(tt <= ss, 1.0, 0.0).astype(BF16)

    for hh in range(group):
        h = hp * group + hh
        i_col = jnp.sum(jnp.where(lane == h, gc, 0.0), axis=1, keepdims=True)
        f_col = jnp.sum(jnp.where(lane == heads + h, gc, 0.0), axis=1, keepdims=True)
        i_row = jnp.sum(jnp.where(sub == h, gr, 0.0), axis=0, keepdims=True)
        f_row = jnp.sum(jnp.where(sub == heads + h, gr, 0.0), axis=0, keepdims=True)
        lf_col = _log_sigmoid(f_col)
        lf_row = _log_sigmoid(f_row)

        lf_col_b = jnp.broadcast_to(lf_col, (L, LANES))
        lf_row_b = jnp.broadcast_to(lf_row, (8, L))
        b_col = sum(jnp.dot(tri, part, preferred_element_type=F32) for part in _split3(lf_col_b))[:, 0:1]
        b_row = sum(jnp.dot(part, tri_t, preferred_element_type=F32) for part in _split3(lf_row_b))[0:1, :]
        b_last = b_col[L - 1:L, :]

        m_prev = m_ref[hh]
        logw = jnp.where(causal, b_col - b_row + i_row, NEG)
        log_inter = b_col + m_prev
        m_t = jnp.maximum(log_inter, jnp.max(logw, axis=-1, keepdims=True))
        w_intra = jnp.exp(logw - m_t) * scale
        w_inter = jnp.exp(log_inter - m_t) * scale

        q = q_ref[0, :, hh * dk:(hh + 1) * dk]
        k = k_ref[0, :, hh * dk:(hh + 1) * dk]
        v = v_ref[0, :, hh * dv:(hh + 1) * dv]
        ct = ct_ref[hh]
        n_row = n_ref[hh]
        a = lax.dot_general(q, k, NT_DIMS, preferred_element_type=F32) * w_intra
        num = (jnp.dot(a.astype(BF16), v, preferred_element_type=F32)
               + w_inter * jnp.dot(q, ct.astype(BF16), preferred_element_type=F32))
        den = (jnp.sum(a, axis=-1, keepdims=True)
               + w_inter * jnp.sum(q.astype(F32) * n_row, axis=-1, keepdims=True))
        hx = num / jnp.maximum(jnp.abs(den), jnp.exp(-m_t))

        m_new = m_t[L - 1:L, :]
        wk = jnp.exp(b_last - b_col + i_col - m_new)
        decay = jnp.exp(b_last + m_prev - m_new)
        kw = k.astype(F32) * wk
        ct_ref[hh] = decay * ct + lax.dot_general(kw.astype(BF16), v, (((0,), (0,)), ((), ())),
                                                  preferred_element_type=F32)
        n_ref[hh] = decay * n_row + jnp.sum(kw, axis=0, keepdims=True)
        m_ref[hh] = m_new

        ms = jnp.mean(hx * hx, axis=-1, keepdims=True)
        hn = hx * lax.rsqrt(ms + EPS) * ng_ref[...]
        og = og_ref[0, :, hh * dv:(hh + 1) * dv].astype(F32)
        sg = sgate_ref[0, :, hh * dv:(hh + 1) * dv].astype(F32)
        o_ref[0, :, hh * dv:(hh + 1) * dv] = (hn * jax.nn.sigmoid(og) * _silu(sg)).astype(o_ref.dtype)


def _mlstm(proj, gates, b_gates, norm_g, *, b, s, heads, chunk, group):
    dk, dv = ML_QK_DIM, ML_V_DIM
    L = _tile(s, chunk)
    ng = 2 * heads
    hg = heads // group
    gates_t = jnp.swapaxes(gates, 1, 2)
    kern = functools.partial(_mlstm_kernel, chunk=L, heads=heads, group=group)
    return pl.pallas_call(
        kern,
        out_shape=jax.ShapeDtypeStruct((b, s, heads * dv), BF16),
        grid=(b, hg, s // L),
        in_specs=[
            pl.BlockSpec((1, L, ng), lambda bi, h, c: (bi, c, 0)),
            pl.BlockSpec((1, ng, L), lambda bi, h, c: (bi, 0, c)),
            pl.BlockSpec((1, ng), lambda bi, h, c: (0, 0)),
            pl.BlockSpec((ng, 1), lambda bi, h, c: (0, 0)),
            pl.BlockSpec((1, dv), lambda bi, h, c: (0, 0)),
            pl.BlockSpec((1, L, group * dk), lambda bi, h, c: (bi, c, h)),
            pl.BlockSpec((1, L, group * dk), lambda bi, h, c: (bi, c, hg + h)),
            pl.BlockSpec((1, L, group * dv), lambda bi, h, c: (bi, c, hg + h)),
            pl.BlockSpec((1, L, group * dv), lambda bi, h, c: (bi, c, 2 * hg + h)),
            pl.BlockSpec((1, L, group * dv), lambda bi, h, c: (bi, c, 3 * hg + h)),
        ],
        out_specs=pl.BlockSpec((1, L, group * dv), lambda bi, h, c: (bi, c, h)),
        scratch_shapes=[pltpu.VMEM((group, dk, dv), F32), pltpu.VMEM((group, 1, dk), F32),
                        pltpu.VMEM((group, 1, 1), F32)],
        compiler_params=_cparams(("parallel", "parallel", "arbitrary")),
        name="mlstm",
    )(gates, gates_t, b_gates.reshape(1, ng).astype(F32), b_gates.reshape(ng, 1).astype(F32),
      norm_g.reshape(1, dv).astype(F32), proj, proj, proj, proj, proj)


def _rope_tables(positions, dim):
    inv_freq = ROPE_THETA ** (-jnp.arange(0, dim, 2, dtype=F32) / dim)
    ang = positions.astype(F32).reshape(-1)[:, None] * inv_freq
    cos, sin = jnp.cos(ang), jnp.sin(ang)
    reps = LANES // dim
    return (jnp.tile(jnp.concatenate([cos, cos], axis=-1), (1, reps)),
            jnp.tile(jnp.concatenate([-sin, sin], axis=-1), (1, reps)))


def kernel(x, positions, norm_g, final_g, da_w_in, da_w_out, da_lambda, da_subln_g, sw_w_in, sw_w_out,
           sw_sinks, ml_w_in, ml_b_gates, ml_w_out, ml_norm_g):
    b, s, d = x.shape
    depth = norm_g.shape[0]
    n = b * s
    cos_a, sin_a = _rope_tables(positions, DA_QK_DIM)
    cos_b, sin_b = _rope_tables(positions, SW_HEAD_DIM)
    da_heads = d // DA_V_DIM
    ml_heads = d // ML_V_DIM
    x2d = x.reshape(n, d).astype(F32)
    tm = 1024

    for i in range(depth):
        kind, j = i % N_MIXERS, i // N_MIXERS
        last = i == depth - 1
        if kind == 0:
            lam_init = 0.8 - 0.6 * math.exp(-0.3 * i)
            nq = 2 * da_heads * DA_QK_DIM
            proj = _norm_proj(x2d, norm_g[i], da_w_in[j].astype(BF16), cos_a, sin_a, tm=tm, tn=2048,
                              q_cols=nq, rope_cols=2 * nq, half=DA_QK_DIM // 2, q_scale=LOG2E * DA_QK_DIM ** -0.5)
            o = _diff_attention(proj.reshape(b, s, -1), da_lambda[j], da_subln_g[j], lam_init,
                                b=b, s=s, heads=da_heads, tq=512, tk=512)
            w_out = da_w_out[j]
        elif kind == 1:
            proj = _norm_proj(x2d, norm_g[i], sw_w_in[j].astype(BF16), cos_b, sin_b, tm=tm, tn=1536,
                              q_cols=d, rope_cols=d + 4 * SW_HEAD_DIM, half=SW_HEAD_DIM // 2,
                              q_scale=SW_HEAD_DIM ** -0.5)
            o = _sliding_window_attention(proj.reshape(b, s, -1), sw_sinks[j], b=b, s=s)
            w_out = sw_w_out[j]
        else:
            w = ml_w_in[j]
            n_main = 2 * ml_heads * ML_QK_DIM + 2 * d
            ng = 2 * ml_heads
            w_main = jnp.concatenate([w[:, :n_main], w[:, n_main + ng:]], axis=1).astype(BF16)
            w_gate = jnp.pad(w[:, n_main:n_main + ng], ((0, 0), (0, LANES - ng))).astype(BF16)
            proj, gates = _norm_proj(x2d, norm_g[i], w_main, cos_a, sin_a, tm=tm, tn=2048, w_side=w_gate)
            o = _mlstm(proj.reshape(b, s, -1), gates[:, :ng].reshape(b, s, ng), ml_b_gates[j], ml_norm_g[j],
                       b=b, s=s, heads=ml_heads, chunk=256, group=4)
            w_out = ml_w_out[j]
        x2d = _out_proj(o.reshape(n, d), w_out.astype(BF16), x2d, final_g, tm=512, final_norm=last)
    return x2d.reshape(b, s, d).astype(x.dtype)
```

```python
import functools
import math
from typing import NamedTuple

import jax
import jax.numpy as jnp
from jax import lax
from jax.experimental import pallas as pl
from jax.experimental.pallas import tpu as pltpu

F32 = jnp.float32
BF16 = jnp.bfloat16

ROPE_THETA = 10000.0
EPS = 1e-6
NEG = -1e30
N_MIXERS = 3

LANES = 128
MXU_N = 256
VMEM_LIMIT = 56 * 1024 * 1024

DA_QK_DIM = 128
DA_V_DIM = 256
SW_HEAD_DIM = 64
SW_GROUP = 8
SW_WINDOW = 128
ML_QK_DIM = 128
ML_V_DIM = 256
ONES_ROWS = 16
LOG2E = math.log2(math.e)

NT_DIMS = (((1,), (1,)), ((), ()))


def _cparams(sem, flags=None):
    return pltpu.CompilerParams(dimension_semantics=sem, vmem_limit_bytes=VMEM_LIMIT, flags=flags)


def _tile(n, pref):
    t = min(n, pref)
    assert n % t == 0, (n, t)
    return t


class _ProjCfg(NamedTuple):
    tn: int
    n_tiles: int
    q_cols: int
    rope_cols: int
    half: int
    q_scale: float
    has_side: bool


def _rope_chunk(xc, cos, sin, half):
    if 2 * half == LANES:
        partner = pltpu.roll(xc, half, 1)
    else:
        lane = lax.broadcasted_iota(jnp.int32, xc.shape, 1)
        partner = jnp.where(lane % (2 * half) < half, pltpu.roll(xc, LANES - half, 1), pltpu.roll(xc, half, 1))
    return xc * cos + partner * sin


def _norm_proj_kernel(*refs, cfg):
    if cfg.has_side:
        x_ref, g_ref, w_ref, ws_ref, cos_ref, sin_ref, o_ref, side_ref, hn_ref = refs
    else:
        x_ref, g_ref, w_ref, cos_ref, sin_ref, o_ref, hn_ref = refs
    j = pl.program_id(1)

    @pl.when(j == 0)
    def _():
        x = x_ref[...]
        ms = jnp.mean(x * x, axis=-1, keepdims=True)
        hn = (x * lax.rsqrt(ms + EPS) * g_ref[...]).astype(BF16)
        hn_ref[...] = hn
        if cfg.has_side:
            side_ref[...] = jnp.dot(hn, ws_ref[...], preferred_element_type=F32)

    def chunk_mode(col):
        return "q" if col < cfg.q_cols else ("k" if col < cfg.rope_cols else None)

    def emit_tile(modes):
        hn = hn_ref[...]
        if any(modes):
            cos = cos_ref[...]
            sin = sin_ref[...]
        if "q" in modes:
            cos_q = cos * cfg.q_scale
            sin_q = sin * cfg.q_scale
        for cc in range(cfg.tn // MXU_N):
            acc = jnp.dot(hn, w_ref[:, cc * MXU_N:(cc + 1) * MXU_N], preferred_element_type=F32)
            for c in range(MXU_N // LANES):
                mode = modes[cc * (MXU_N // LANES) + c]
                xc = acc[:, c * LANES:(c + 1) * LANES]
                if mode == "q":
                    xc = _rope_chunk(xc, cos_q, sin_q, cfg.half)
                elif mode == "k":
                    xc = _rope_chunk(xc, cos, sin, cfg.half)
                lo = cc * MXU_N + c * LANES
                o_ref[:, lo:lo + LANES] = xc.astype(o_ref.dtype)

    tile_modes = [tuple(chunk_mode(jj * cfg.tn + c * LANES) for c in range(cfg.tn // LANES))
                  for jj in range(cfg.n_tiles)]
    start = 0
    for jj in range(1, cfg.n_tiles + 1):
        if jj == cfg.n_tiles or tile_modes[jj] != tile_modes[start]:
            pl.when((j >= start) & (j < jj))(functools.partial(emit_tile, tile_modes[start]))
            start = jj


def _norm_proj(x2d, g, w, cos, sin, *, tm, tn, q_cols=0, rope_cols=0, half=64, q_scale=1.0, w_side=None):
    n, d = x2d.shape
    dout = w.shape[1]
    tm = _tile(n, tm)
    assert dout % tn == 0 and tn % MXU_N == 0
    cfg = _ProjCfg(tn=tn, n_tiles=dout // tn, q_cols=q_cols, rope_cols=max(rope_cols, q_cols), half=half,
                   q_scale=q_scale, has_side=w_side is not None)
    in_specs = [
        pl.BlockSpec((tm, d), lambda i, j: (i, 0)),
        pl.BlockSpec((1, d), lambda i, j: (0, 0)),
        pl.BlockSpec((d, tn), lambda i, j: (0, j)),
    ]
    args = [x2d, g.reshape(1, d), w]
    out_shape = [jax.ShapeDtypeStruct((n, dout), BF16)]
    out_specs = [pl.BlockSpec((tm, tn), lambda i, j: (i, j))]
    if cfg.has_side:
        in_specs.append(pl.BlockSpec((d, LANES), lambda i, j: (0, 0)))
        args.append(w_side)
        out_shape.append(jax.ShapeDtypeStruct((n, LANES), F32))
        out_specs.append(pl.BlockSpec((tm, LANES), lambda i, j: (i, 0)))
    in_specs += [pl.BlockSpec((tm, LANES), lambda i, j: (i, 0))] * 2
    args += [cos, sin]
    res = pl.pallas_call(
        functools.partial(_norm_proj_kernel, cfg=cfg),
        out_shape=out_shape,
        grid=(n // tm, dout // tn),
        in_specs=in_specs,
        out_specs=out_specs,
        scratch_shapes=[pltpu.VMEM((tm, d), BF16)],
        compiler_params=_cparams(("parallel", "arbitrary")),
        name="norm_proj",
    )(*args)
    return res if cfg.has_side else res[0]


def _out_proj_kernel(*refs, final_norm, gated):
    if gated:
        o_ref, og_ref, sg_ref, w_ref, x_ref, fg_ref, out_ref = refs
        rc = MXU_N
        ys = []
        for r in range(o_ref.shape[0] // rc):
            rows = slice(r * rc, (r + 1) * rc)
            o = (o_ref[rows, :].astype(F32) * jax.nn.sigmoid(og_ref[rows, :].astype(F32))
                 * _silu(sg_ref[rows, :].astype(F32))).astype(BF16)
            ys.append(jnp.dot(o, w_ref[...], preferred_element_type=F32))
        y = jnp.concatenate(ys, axis=0)
    else:
        o_ref, w_ref, x_ref, fg_ref, out_ref = refs
        y = jnp.dot(o_ref[...], w_ref[...], preferred_element_type=F32)
    xn = x_ref[...] + y
    if final_norm:
        ms = jnp.mean(xn * xn, axis=-1, keepdims=True)
        xn = xn * lax.rsqrt(ms + EPS) * fg_ref[...]
    out_ref[...] = xn


def _out_proj(o2d, w, x2d, final_g, *, tm, final_norm, gate_src=None, gate_blocks=None):
    n, d = x2d.shape
    k = o2d.shape[1]
    tm = _tile(n, tm)
    gated = gate_src is not None
    in_specs = [pl.BlockSpec((tm, k), lambda i: (i, 0))]
    args = [o2d]
    if gated:
        ga, gb = gate_blocks
        in_specs += [pl.BlockSpec((tm, k), lambda i: (i, ga)), pl.BlockSpec((tm, k), lambda i: (i, gb))]
        args += [gate_src, gate_src]
    in_specs += [
        pl.BlockSpec((k, d), lambda i: (0, 0)),
        pl.BlockSpec((tm, d), lambda i: (i, 0)),
        pl.BlockSpec((1, d), lambda i: (0, 0)),
    ]
    args += [w, x2d, final_g.reshape(1, d)]
    return pl.pallas_call(
        functools.partial(_out_proj_kernel, final_norm=final_norm, gated=gated),
        out_shape=jax.ShapeDtypeStruct((n, d), F32),
        grid=(n // tm,),
        in_specs=in_specs,
        out_specs=pl.BlockSpec((tm, d), lambda i: (i, 0)),
        compiler_params=_cparams(("parallel",)),
        name="out_proj",
    )(*args)


def _silu(g):
    return g * jax.nn.sigmoid(g)


def _da_kernel(lam_ref, sg_ref, q_ref, k_ref, v_ref, g_ref, o_ref, vt_ref, acc_ref, s_ref, p_ref, *,
               tq, tk, lam_init):
    i = pl.program_id(2)
    d = DA_QK_DIM
    dv = DA_V_DIM

    @pl.when(i == 0)
    def _():
        for c in range(vt_ref.shape[0]):
            vt_ref[c, :dv, :] = v_ref[0, c * tk:(c + 1) * tk, :].astype(F32).T.astype(BF16)
            vt_ref[c, dv:, :] = jnp.ones((ONES_ROWS, tk), BF16)

    nqb = q_ref.shape[1] // tq

    def q_block(qi):
        return q_ref[0, pl.ds(pl.multiple_of(qi * tq, tq), tq), :]

    acc_ref[...] = jnp.zeros_like(acc_ref)

    def qk(j, buf, q):
        off = pl.multiple_of(j * tk, tk)
        for mp in range(2):
            kb = k_ref[0, pl.ds(off, tk), mp * d:(mp + 1) * d]
            s_ref[buf, mp] = lax.dot_general(kb, q[:, mp * d:(mp + 1) * d], NT_DIMS,
                                             preferred_element_type=F32)

    def softmax(j, buf, mp, m, masked):
        if masked:
            key = j * tk + lax.broadcasted_iota(jnp.int32, (tk, tq), 0)
            qry = i * tq + lax.broadcasted_iota(jnp.int32, (tk, tq), 1)
            s_ref[buf, mp] = jnp.where(key <= qry, s_ref[buf, mp], NEG)
        m_new = jnp.maximum(m, jnp.max(s_ref[buf, mp], axis=0, keepdims=True))
        p_ref[buf, mp] = jnp.exp2(s_ref[buf, mp] - m_new).astype(BF16)
        return m_new, jnp.exp2(m - m_new)

    def pv(j, buf, mp, alpha):
        acc_ref[mp] = acc_ref[mp] * alpha + jnp.dot(vt_ref[j], p_ref[buf, mp], preferred_element_type=F32)

    def step(j, ms, buf, last):
        if last:
            qk(0, 1 - buf, q_block(jnp.minimum(i + 1, nqb - 1)))
        else:
            qk(j + 1, 1 - buf, q_cur)
        new = []
        for mp in range(2):
            m_new, alpha = softmax(j, buf, mp, ms[mp], last)
            pv(j, buf, mp, alpha)
            new.append(m_new)
        return tuple(new)

    base = ((i * (i + 1)) // 2) % 2

    def any_step(j, ms):
        return lax.cond((j + base) % 2 == 0, lambda ms: step(j, ms, 0, False), lambda ms: step(j, ms, 1, False), ms)

    n_full = i
    q_cur = q_block(i)

    @pl.when(i == 0)
    def _():
        qk(0, 0, q_cur)

    m0 = jnp.full((1, tq), NEG, F32)
    ms = lax.fori_loop(0, n_full, any_step, (m0, m0))
    for parity in range(2):
        @pl.when((n_full + base) % 2 == parity)
        def _(parity=parity):
            step(n_full, ms, parity, True)

    lam = lam_ref[...]
    lam_full = (jnp.exp(jnp.sum(lam[0:1] * lam[1:2], axis=-1, keepdims=True))
                - jnp.exp(jnp.sum(lam[2:3] * lam[3:4], axis=-1, keepdims=True)) + lam_init)
    inv1 = 1.0 / acc_ref[0, dv:dv + 1, :]
    inv2 = 1.0 / acc_ref[1, dv:dv + 1, :]
    o_t = acc_ref[0, :dv, :] * inv1 - lam_full * (acc_ref[1, :dv, :] * inv2)
    o = o_t.T
    ms = jnp.mean(o * o, axis=-1, keepdims=True)
    o = o * lax.rsqrt(ms + EPS) * sg_ref[...] * (1.0 - lam_init)
    o = o * _silu(g_ref[0].astype(F32))
    o_ref[0] = o.astype(o_ref.dtype)


def _diff_attention(proj, lam, subln_g, lam_init, *, b, s, heads, tq, tk):
    dv = DA_V_DIM
    tq, tk = _tile(s, tq), _tile(s, tk)
    assert tk == tq
    kern = functools.partial(_da_kernel, tq=tq, tk=tk, lam_init=lam_init)
    return pl.pallas_call(
        kern,
        out_shape=jax.ShapeDtypeStruct((b, s, heads * dv), BF16),
        grid=(b, heads, s // tq),
        in_specs=[
            pl.BlockSpec((4, DA_QK_DIM), lambda bi, h, i: (0, 0)),
            pl.BlockSpec((1, dv), lambda bi, h, i: (0, 0)),
            pl.BlockSpec((1, s, dv), lambda bi, h, i: (bi, 0, h)),
            pl.BlockSpec((1, s, dv), lambda bi, h, i: (bi, 0, heads + h)),
            pl.BlockSpec((1, s, dv), lambda bi, h, i: (bi, 0, 2 * heads + h)),
            pl.BlockSpec((1, tq, dv), lambda bi, h, i: (bi, i, 3 * heads + h)),
        ],
        out_specs=pl.BlockSpec((1, tq, dv), lambda bi, h, i: (bi, i, h)),
        scratch_shapes=[pltpu.VMEM((s // tk, dv + ONES_ROWS, tk), BF16), pltpu.VMEM((2, dv + ONES_ROWS, tq), F32),
                        pltpu.VMEM((2, 2, tk, tq), F32), pltpu.VMEM((2, 2, tk, tq), BF16)],
        compiler_params=_cparams(("parallel", "parallel", "arbitrary")),
        name="diff_attention",
    )(lam.astype(F32), subln_g.reshape(1, dv).astype(F32), proj, proj, proj, proj)


def _swa_kernel(sink_ref, q_ref, kvc_ref, kvp_ref, g0_ref, g1_ref, g2_ref, g3_ref, o_ref):
    i = pl.program_id(1)
    w = SW_WINDOW
    hd = SW_HEAD_DIM
    nkv = 4
    g_refs = (g0_ref, g1_ref, g2_ref, g3_ref)
    has_prev = i > 0

    lane = lax.broadcasted_iota(jnp.int32, (w, LANES), 1)
    lo = lane < hd
    rows = 4 * w
    row = lax.broadcasted_iota(jnp.int32, (rows, w), 0) % w
    col = lax.broadcasted_iota(jnp.int32, (rows, w), 1)
    lower = col <= row
    ones_w = jnp.ones((w, LANES), BF16)

    def split_pair(x_bf16, parity):
        xf = x_bf16.astype(F32)
        if parity == 0:
            a = jnp.where(lo, xf, 0.0)
            bb = pltpu.roll(a, hd, 1)
        else:
            bb = jnp.where(lo, 0.0, xf)
            a = pltpu.roll(bb, hd, 1)
        return a.astype(BF16), bb.astype(BF16)

    kvc = kvc_ref[0]
    kvp = kvp_ref[0]
    q_all = q_ref[0]
    for kvh in range(nkv):
        c, parity = kvh // 2, kvh % 2
        kc = split_pair(kvc[:, c * LANES:(c + 1) * LANES], parity)
        kp = split_pair(kvp[:, c * LANES:(c + 1) * LANES], parity)
        vc = split_pair(kvc[:, (2 + c) * LANES:(3 + c) * LANES], parity)
        vp = split_pair(kvp[:, (2 + c) * LANES:(3 + c) * LANES], parity)
        base = kvh * SW_GROUP * hd
        q4 = jnp.concatenate([q_all[:, base + jj * LANES: base + (jj + 1) * LANES] for jj in range(4)], axis=0)
        o4 = jnp.zeros((rows, LANES), F32)
        for ab in range(2):
            k2 = jnp.concatenate([kp[ab], kc[ab]], axis=0)
            v2 = jnp.concatenate([vp[ab], vc[ab]], axis=0)
            s2 = lax.dot_general(q4, k2, NT_DIMS, preferred_element_type=F32)
            s = jnp.where(lower, s2[:, w:], jnp.where(has_prev, s2[:, :w], NEG))
            sink = jnp.concatenate(
                [jnp.full((w, LANES), sink_ref[kvh * SW_GROUP + 2 * jj + ab], F32) for jj in range(4)], axis=0)
            m = jnp.maximum(jnp.max(s, axis=-1, keepdims=True), sink)
            e = jnp.exp(s - m)
            p2 = jnp.concatenate([jnp.where(lower, 0.0, e), jnp.where(lower, e, 0.0)], axis=1).astype(BF16)
            denom = jnp.dot(e.astype(BF16), ones_w, preferred_element_type=F32) + jnp.exp(sink - m)
            o4 = o4 + jnp.dot(p2, v2, preferred_element_type=F32) * (1.0 / denom)
        gate = g_refs[kvh][0].astype(F32)
        for jj in range(4):
            og = o4[jj * w:(jj + 1) * w] * _silu(gate[:, jj * LANES:(jj + 1) * LANES])
            o_ref[0, :, base + jj * LANES: base + (jj + 1) * LANES] = og.astype(o_ref.dtype)


def _sliding_window_attention(proj, sinks, *, b, s):
    w = SW_WINDOW
    width = 32 * SW_HEAD_DIM
    gw = SW_GROUP * SW_HEAD_DIM
    kv_blk = width // gw
    g_specs = [pl.BlockSpec((1, w, gw), functools.partial(lambda bi, i, kk: (bi, i, kk), kk=kv_blk + 1 + kvh))
               for kvh in range(4)]
    return pl.pallas_call(
        _swa_kernel,
        out_shape=jax.ShapeDtypeStruct((b, s, width), BF16),
        grid=(b, s // w),
        in_specs=[
            pl.BlockSpec(memory_space=pltpu.SMEM),
            pl.BlockSpec((1, w, width), lambda bi, i: (bi, i, 0)),
            pl.BlockSpec((1, w, gw), lambda bi, i: (bi, i, kv_blk)),
            pl.BlockSpec((1, w, gw), lambda bi, i: (bi, jnp.maximum(i - 1, 0), kv_blk)),
        ] + g_specs,
        out_specs=pl.BlockSpec((1, w, width), lambda bi, i: (bi, i, 0)),
        compiler_params=_cparams(("parallel", "arbitrary")),
        name="sliding_window_attention",
    )(sinks.astype(F32), proj, proj, proj, proj, proj, proj, proj)


def _log_sigmoid(x):
    return jnp.minimum(x, 0.0) - jnp.log1p(jnp.exp(-jnp.abs(x)))


def _split3(x):
    hi = x.astype(BF16)
    r1 = x - hi.astype(F32)
    mid = r1.astype(BF16)
    lo = (r1 - mid.astype(F32)).astype(BF16)
    return hi, mid, lo


def _mlstm_kernel(gcol_ref, grow_ref, bcol_ref, brow_ref, ng_ref, q_ref, k_ref, v_ref,
                  o_ref, ct_ref, n_ref, m_ref, *, chunk, heads, group):
    hp = pl.program_id(1)
    c = pl.program_id(2)
    L = chunk
    dk, dv = ML_QK_DIM, ML_V_DIM
    scale = dk ** -0.5

    @pl.when(c == 0)
    def _():
        ct_ref[...] = jnp.zeros_like(ct_ref)
        n_ref[...] = jnp.zeros_like(n_ref)
        m_ref[...] = jnp.zeros_like(m_ref)

    gc = gcol_ref[0] + bcol_ref[...]
    gr = grow_ref[0] + brow_ref[...]
    lane = lax.broadcasted_iota(jnp.int32, gc.shape, 1)
    sub = lax.broadcasted_iota(jnp.int32, gr.shape, 0)
    tt = lax.broadcasted_iota(jnp.int32, (L, L), 0)
    ss = lax.broadcasted_iota(jnp.int32, (L, L), 1)
    causal = ss <= tt
    tri = jnp.where(causal, 1.0, 0.0).astype(BF16)
    tri_t = jnp.where(tt <= ss, 1.0, 0.0).astype(BF16)

    for hh in range(group):
        h = hp * group + hh
        i_col = jnp.sum(jnp.where(lane == h, gc, 0.0), axis=1, keepdims=True)
        f_col = jnp.sum(jnp.where(lane == heads + h, gc, 0.0), axis=1, keepdims=True)
        i_row = jnp.sum(jnp.where(sub == h, gr, 0.0), axis=0, keepdims=True)
        f_row = jnp.sum(jnp.where(sub == heads + h, gr, 0.0), axis=0, keepdims=True)
        lf_col = _log_sigmoid(f_col)
        lf_row = _log_sigmoid(f_row)

        lf_col_b = jnp.broadcast_to(lf_col, (L, LANES))
        lf_row_b = jnp.broadcast_to(lf_row, (8, L))
        b_col = sum(jnp.dot(tri, part, preferred_element_type=F32) for part in _split3(lf_col_b))[:, 0:1]
        b_row = sum(jnp.dot(part, tri_t, preferred_element_type=F32) for part in _split3(lf_row_b))[0:1, :]
        b_last = b_col[L - 1:L, :]

        m_prev = m_ref[hh]
        logw = jnp.where(causal, b_col - b_row + i_row, NEG)
        log_inter = b_col + m_prev
        m_t = jnp.maximum(log_inter, jnp.max(logw, axis=-1, keepdims=True))
        w_intra = jnp.exp(logw - m_t) * scale
        w_inter = jnp.exp(log_inter - m_t) * scale

        q = q_ref[0, :, hh * dk:(hh + 1) * dk]
        k = k_ref[0, :, hh * dk:(hh + 1) * dk]
        v = v_ref[0, :, hh * dv:(hh + 1) * dv]
        ct = ct_ref[hh]
        n_row = n_ref[hh]
        a = lax.dot_general(q, k, NT_DIMS, preferred_element_type=F32) * w_intra
        num = (jnp.dot(a.astype(BF16), v, preferred_element_type=F32)
               + w_inter * jnp.dot(q, ct.astype(BF16), preferred_element_type=F32))
        den = (jnp.sum(a, axis=-1, keepdims=True)
               + w_inter * jnp.sum(q.astype(F32) * n_row, axis=-1, keepdims=True))
        hx = num / jnp.maximum(jnp.abs(den), jnp.exp(-m_t))

        m_new = m_t[L - 1:L, :]
        wk = jnp.exp(b_last - b_col + i_col - m_new)
        decay = jnp.exp(b_last + m_prev - m_new)
        kw = k.astype(F32) * wk
        ct_ref[hh] = decay * ct + lax.dot_general(kw.astype(BF16), v, (((0,), (0,)), ((), ())),
                                                  preferred_element_type=F32)
        n_ref[hh] = decay * n_row + jnp.sum(kw, axis=0, keepdims=True)
        m_ref[hh] = m_new

        ms = jnp.mean(hx * hx, axis=-1, keepdims=True)
        hn = hx * lax.rsqrt(ms + EPS) * ng_ref[...]
        o_ref[0, :, hh * dv:(hh + 1) * dv] = hn.astype(o_ref.dtype)


def _mlstm(proj, gates, b_gates, norm_g, *, b, s, heads, chunk, group):
    dk, dv = ML_QK_DIM, ML_V_DIM
    L = _tile(s, chunk)
    ng = 2 * heads
    hg = heads // group
    gates_t = jnp.swapaxes(gates, 1, 2)
    kern = functools.partial(_mlstm_kernel, chunk=L, heads=heads, group=group)
    return pl.pallas_call(
        kern,
        out_shape=jax.ShapeDtypeStruct((b, s, heads * dv), BF16),
        grid=(b, hg, s // L),
        in_specs=[
            pl.BlockSpec((1, L, ng), lambda bi, h, c: (bi, c, 0)),
            pl.BlockSpec((1, ng, L), lambda bi, h, c: (bi, 0, c)),
            pl.BlockSpec((1, ng), lambda bi, h, c: (0, 0)),
            pl.BlockSpec((ng, 1), lambda bi, h, c: (0, 0)),
            pl.BlockSpec((1, dv), lambda bi, h, c: (0, 0)),
            pl.BlockSpec((1, L, group * dk), lambda bi, h, c: (bi, c, h)),
            pl.BlockSpec((1, L, group * dk), lambda bi, h, c: (bi, c, hg + h)),
            pl.BlockSpec((1, L, group * dv), lambda bi, h, c: (bi, c, hg + h)),
        ],
        out_specs=pl.BlockSpec((1, L, group * dv), lambda bi, h, c: (bi, c, h)),
        scratch_shapes=[pltpu.VMEM((group, dk, dv), F32), pltpu.VMEM((group, 1, dk), F32),
                        pltpu.VMEM((group, 1, 1), F32)],
        compiler_params=_cparams(("parallel", "parallel", "arbitrary")),
        name="mlstm",
    )(gates, gates_t, b_gates.reshape(1, ng).astype(F32), b_gates.reshape(ng, 1).astype(F32),
      norm_g.reshape(1, dv).astype(F32), proj, proj, proj)


def _rope_tables(positions, dim):
    inv_freq = ROPE_THETA ** (-jnp.arange(0, dim, 2, dtype=F32) / dim)
    ang = positions.astype(F32).reshape(-1)[:, None] * inv_freq
    cos, sin = jnp.cos(ang), jnp.sin(ang)
    reps = LANES // dim
    return (jnp.tile(jnp.concatenate([cos, cos], axis=-1), (1, reps)),
            jnp.tile(jnp.concatenate([-sin, sin], axis=-1), (1, reps)))


def kernel(x, positions, norm_g, final_g, da_w_in, da_w_out, da_lambda, da_subln_g, sw_w_in, sw_w_out,
           sw_sinks, ml_w_in, ml_b_gates, ml_w_out, ml_norm_g):
    b, s, d = x.shape
    depth = norm_g.shape[0]
    n = b * s
    cos_a, sin_a = _rope_tables(positions, DA_QK_DIM)
    cos_b, sin_b = _rope_tables(positions, SW_HEAD_DIM)
    da_heads = d // DA_V_DIM
    ml_heads = d // ML_V_DIM
    x2d = x.reshape(n, d).astype(F32)
    tm = 1024

    for i in range(depth):
        kind, j = i % N_MIXERS, i // N_MIXERS
        last = i == depth - 1
        if kind == 0:
            lam_init = 0.8 - 0.6 * math.exp(-0.3 * i)
            nq = 2 * da_heads * DA_QK_DIM
            proj = _norm_proj(x2d, norm_g[i], da_w_in[j].astype(BF16), cos_a, sin_a, tm=tm, tn=2048,
                              q_cols=nq, rope_cols=2 * nq, half=DA_QK_DIM // 2, q_scale=LOG2E * DA_QK_DIM ** -0.5)
            o = _diff_attention(proj.reshape(b, s, -1), da_lambda[j], da_subln_g[j], lam_init,
                                b=b, s=s, heads=da_heads, tq=512, tk=512)
            w_out = da_w_out[j]
        elif kind == 1:
            proj = _norm_proj(x2d, norm_g[i], sw_w_in[j].astype(BF16), cos_b, sin_b, tm=tm, tn=1536,
                              q_cols=d, rope_cols=d + 4 * SW_HEAD_DIM, half=SW_HEAD_DIM // 2,
                              q_scale=SW_HEAD_DIM ** -0.5)
            o = _sliding_window_attention(proj.reshape(b, s, -1), sw_sinks[j], b=b, s=s)
            w_out = sw_w_out[j]
        else:
            w = ml_w_in[j]
            n_main = 2 * ml_heads * ML_QK_DIM + 2 * d
            ng = 2 * ml_heads
            w_main = jnp.concatenate([w[:, :n_main], w[:, n_main + ng:]], axis=1).astype(BF16)
            w_gate = jnp.pad(w[:, n_main:n_main + ng], ((0, 0), (0, LANES - ng))).astype(BF16)
            proj, gates = _norm_proj(x2d, norm_g[i], w_main, cos_a, sin_a, tm=tm, tn=2048, w_side=w_gate)
            o = _mlstm(proj.reshape(b, s, -1), gates[:, :ng].reshape(b, s, ng), ml_b_gates[j], ml_norm_g[j],
                       b=b, s=s, heads=ml_heads, chunk=256, group=4)
            x2d = _out_proj(o.reshape(n, d), ml_w_out[j].astype(BF16), x2d, final_g, tm=512, final_norm=last,
                            gate_src=proj, gate_blocks=(2, 3))
            continue
        x2d = _out_proj(o.reshape(n, d), w_out.astype(BF16), x2d, final_g, tm=512, final_norm=last)
    return x2d.reshape(b, s, d).astype(x.dtype)
```
